```python
import math
import functools
import jax
import jax.numpy as jnp
from jax import lax
import numpy as np

D_MODEL = 1024
BATCH = 8
SEQ = 4096
DEPTH = 2
DEC_BATCH = 32
DEC_SEQ = 1
PAST_LEN = 16384
PAGE_SIZE = 128

BR_W = 512
A_HEADS = 4
A_DQK = 64
A_DV = 2 * A_DQK
M_HEADS = 4
M_DH = 128
CONV_W = 4
M_CHUNK = 128
N_MEM = 256
X_HEADS = 4
X_DH = 128
N_BRANCH = 3
D_FF = 2816
N_EXPERTS = 8
TOP_K = 2
D_FF_E = 1408
N_DENSE = (DEPTH + 1) // 2
N_MOE = DEPTH // 2
Q_BLOCK = 128
EPS = 1e-6
NEG_BIG = -1e30

OFF_AQ = 0
OFF_AK = OFF_AQ + A_HEADS * 2 * A_DQK
OFF_AV = OFF_AK + A_HEADS * 2 * A_DQK
OFF_MQK = OFF_AV + A_HEADS * A_DV
OFF_MV = OFF_MQK + 2 * M_HEADS * M_DH
OFF_MO = OFF_MV + M_HEADS * M_DH
OFF_MIF = OFF_MO + M_HEADS * M_DH
OFF_XQ = OFF_MIF + 2 * M_HEADS
OFF_G = OFF_XQ + X_HEADS * X_DH
IN_COLS = OFF_G + N_BRANCH * D_MODEL

kernel_name = 'hybrid_diffattn_mlstm_memxattn_step'


def rms_norm(x, g):
    xf = x.astype(jnp.float32)
    y = xf * lax.rsqrt(jnp.mean(xf * xf, axis=-1, keepdims=True) + EPS)
    return (y * g.astype(jnp.float32)).astype(x.dtype)


def diff_lambda(lq1, lk1, lq2, lk2, lam_init):
    f32 = jnp.float32
    return (jnp.exp(jnp.sum(lq1.astype(f32) * lk1.astype(f32)))
            - jnp.exp(jnp.sum(lq2.astype(f32) * lk2.astype(f32))) + lam_init)


def diff_scores(q, k):
    return jnp.einsum('bqhmd,bkhmd->bhmqk', q, k.astype(q.dtype)).astype(jnp.float32) * (A_DQK ** -0.5)


def diff_attn_prompt(q, k, v, lam):
    b, t = q.shape[:2]
    qb = min(Q_BLOCK, t)
    nb = t // qb
    q_blocks = jnp.moveaxis(q.reshape(b, nb, qb, A_HEADS, 2, A_DQK), 1, 0)
    k_pos = jnp.arange(t)

    def one_block(args):
        i, qi = args
        q_pos = i * qb + jnp.arange(qb)
        s = jnp.where(q_pos[:, None] >= k_pos[None, :], diff_scores(qi, k), -jnp.inf)
        p = jax.nn.softmax(s, axis=-1)
        pd = p[:, :, 0] - lam * p[:, :, 1]
        return jnp.einsum('bhqk,bkhd->bqhd', pd.astype(v.dtype), v)

    out = lax.map(one_block, (jnp.arange(nb), q_blocks))
    return jnp.moveaxis(out, 0, 1).reshape(b, t, A_HEADS, A_DV)


def diff_attn_sample(q, k_new, v_new, lam, cache_k, cache_v, page_table, layer):
    b, t = q.shape[:2]
    k_past = cache_k[layer, page_table].reshape(b, -1, A_HEADS, 2, A_DQK)
    v_past = cache_v[layer, page_table].reshape(b, -1, A_HEADS, A_DV)
    n_past = k_past.shape[1]
    causal = jnp.tril(jnp.ones((t, t), dtype=bool))
    s = jnp.concatenate([diff_scores(q, k_past),
                         jnp.where(causal, diff_scores(q, k_new), -jnp.inf)], axis=-1)
    p = jax.nn.softmax(s, axis=-1)
    pd = (p[:, :, 0] - lam * p[:, :, 1]).astype(v_new.dtype)
    return (jnp.einsum('bhqk,bkhd->bqhd', pd[..., :n_past], v_past.astype(v_new.dtype))
            + jnp.einsum('bhqk,bkhd->bqhd', pd[..., n_past:], v_new))


def mem_kv(mem, g_norm, w_kv, g_k):
    b, n, _ = mem.shape
    kv = rms_norm(mem, g_norm) @ w_kv
    k = rms_norm(kv[..., :X_HEADS * X_DH].reshape(b, n, X_HEADS, X_DH), g_k)
    v = kv[..., X_HEADS * X_DH:].reshape(b, n, X_HEADS, X_DH)
    return k, v


def mem_attend(q, k, v):
    s = jnp.einsum('bqhd,bkhd->bhqk', q, k.astype(q.dtype)).astype(jnp.float32) * (X_DH ** -0.5)
    p = jax.nn.softmax(s, axis=-1)
    return jnp.einsum('bhqk,bkhd->bqhd', p.astype(v.dtype), v)


def causal_conv(u, buf, w, bias):
    t = u.shape[1]
    up = jnp.concatenate([buf.astype(u.dtype), u], axis=1)
    y = bias
    for j in range(CONV_W):
        y = y + up[:, j:j + t] * w[j]
    return y, up[:, up.shape[1] - (CONV_W - 1):]


def mlstm_chunk(carry, xs):
    c_prev, n_prev, m_prev = carry
    q, k, v, li, lf = xs
    L = q.shape[2]
    bcum = jnp.cumsum(lf, axis=-1)
    causal = jnp.tril(jnp.ones((L, L), dtype=bool))
    log_d = jnp.where(causal, bcum[..., :, None] - bcum[..., None, :] + li[..., None, :], -jnp.inf)
    log_inter = bcum + m_prev[..., None]
    m_t = jnp.maximum(log_inter, jnp.max(log_d, axis=-1))
    d = jnp.exp(log_d - m_t[..., None])
    w_inter = jnp.exp(log_inter - m_t)
    s = jnp.einsum('bhtd,bhsd->bhts', q, k) * d
    num = (w_inter[..., None] * jnp.einsum('bhvd,bhtd->bhtv', c_prev, q)
           + jnp.einsum('bhts,bhsv->bhtv', s, v))
    den = w_inter * jnp.einsum('bhd,bhtd->bht', n_prev, q) + jnp.sum(s, axis=-1)
    h = num / jnp.maximum(jnp.abs(den), jnp.exp(-m_t))[..., None]
    m_new = m_t[..., -1]
    w_c = jnp.exp(bcum[..., -1] + m_prev - m_new)
    w_s = jnp.exp(bcum[..., -1:] - bcum + li - m_new[..., None])
    c_new = w_c[..., None, None] * c_prev + jnp.einsum('bhs,bhsv,bhsd->bhvd', w_s, v, k)
    n_new = w_c[..., None] * n_prev + jnp.einsum('bhs,bhsd->bhd', w_s, k)
    return (c_new, n_new, m_new), h


def mlstm_scan(q, k, v, li, lf, c0, n0, m0):
    b, t, nh, _ = q.shape
    L = min(M_CHUNK, t)
    nc = -(-t // L)
    pad = nc * L - t

    def prep(a, fill):
        a = jnp.moveaxis(a.astype(jnp.float32), 2, 1)
        if pad:
            widths = [(0, 0)] * a.ndim
            widths[2] = (0, pad)
            a = jnp.pad(a, widths, constant_values=fill)
        a = a.reshape(a.shape[:2] + (nc, L) + a.shape[3:])
        return jnp.moveaxis(a, 2, 0)

    xs = (prep(q, 0.0), prep(k, 0.0), prep(v, 0.0), prep(li, NEG_BIG), prep(lf, 0.0))
    carry0 = (c0.astype(jnp.float32), n0.astype(jnp.float32), m0.astype(jnp.float32))
    (c_f, n_f, m_f), h = lax.scan(mlstm_chunk, carry0, xs)
    h = jnp.moveaxis(h, 0, 2).reshape(b, nh, nc * L, -1)[:, :, :t]
    return jnp.moveaxis(h, 1, 2), c_f, n_f, m_f


def token_mixer(h, p, lam, lam_init, attn_fn, mk, mv, conv_buf, c0, n0, m0):
    b, t, _ = h.shape
    z = h @ p['w_in']
    qa = rms_norm(z[..., OFF_AQ:OFF_AK].reshape(b, t, A_HEADS, 2, A_DQK), p['g_q'])
    ka = rms_norm(z[..., OFF_AK:OFF_AV].reshape(b, t, A_HEADS, 2, A_DQK), p['g_k'])
    va = z[..., OFF_AV:OFF_MQK].reshape(b, t, A_HEADS, A_DV)
    oa = rms_norm(attn_fn(qa, ka, va, lam), p['g_subln']) * (1.0 - lam_init)
    cq, conv_new = causal_conv(z[..., OFF_MQK:OFF_MV], conv_buf, p['w_conv'], p['b_conv'])
    cq = jax.nn.silu(cq)
    qm = cq[..., :M_HEADS * M_DH].reshape(b, t, M_HEADS, M_DH)
    km = cq[..., M_HEADS * M_DH:].reshape(b, t, M_HEADS, M_DH) * (M_DH ** -0.5)
    vm = z[..., OFF_MV:OFF_MO].reshape(b, t, M_HEADS, M_DH)
    om = jax.nn.sigmoid(z[..., OFF_MO:OFF_MIF])
    gif = z[..., OFF_MIF:OFF_XQ].astype(jnp.float32) + p['b_if'].astype(jnp.float32)
    li = gif[..., :M_HEADS]
    lf = jax.nn.log_sigmoid(gif[..., M_HEADS:])
    hm, c_new, n_new, m_new = mlstm_scan(qm, km, vm, li, lf, c0, n0, m0)
    hm = om * rms_norm(hm.astype(h.dtype), p['g_mh']).reshape(b, t, M_HEADS * M_DH)
    qx = rms_norm(z[..., OFF_XQ:OFF_G].reshape(b, t, X_HEADS, X_DH), p['g_mq'])
    ox = mem_attend(qx, mk, mv).reshape(b, t, X_HEADS * X_DH)
    gates = jax.nn.sigmoid(z[..., OFF_G:].reshape(b, t, N_BRANCH, D_MODEL))
    branches = jnp.stack([oa.reshape(b, t, BR_W), hm, ox], axis=2)
    merged = jnp.sum(gates * jnp.einsum('btnc,ncd->btnd', branches, p['w_branch']), axis=2)
    return merged @ p['w_o'], ka, va, conv_new, c_new, n_new, m_new


def swiglu(h, w_gu, w_down):
    g, u = jnp.split(h @ w_gu, 2, axis=-1)
    return (jax.nn.silu(g) * u) @ w_down


def moe_swiglu(h, w_router, b_router, w_gu, w_down):
    logits = (h @ w_router).astype(jnp.float32) + b_router.astype(jnp.float32)
    top_v, top_i = lax.top_k(logits, TOP_K)
    top_w = jax.nn.softmax(top_v, axis=-1)
    gate = jnp.sum(jax.nn.one_hot(top_i, N_EXPERTS, dtype=jnp.float32) * top_w[..., None], axis=-2)
    out = jnp.zeros(h.shape, jnp.float32)
    for e in range(N_EXPERTS):
        out = out + gate[..., e:e + 1] * swiglu(h, w_gu[e], w_down[e]).astype(jnp.float32)
    return out.astype(h.dtype)


def setup_inputs(seed: int = 0) -> dict:
    key = jax.random.key(seed)
    ks = iter(jax.random.split(key, 48))
    f32 = jnp.float32

    def nrm(shape, scale=1.0):
        return scale * jax.random.normal(next(ks), shape, f32)

    def gain(shape):
        return 1.0 + 0.02 * jax.random.normal(next(ks), shape, f32)

    n_pages = PAST_LEN // PAGE_SIZE
    n_used = DEC_BATCH * n_pages
    n_pool = n_used + max(1, n_used // 4)
    mw2 = 2 * M_HEADS * M_DH
    x_prompt = nrm((BATCH, SEQ, D_MODEL))
    x_sample = nrm((DEC_BATCH, DEC_SEQ, D_MODEL))
    mem_prompt = nrm((BATCH, N_MEM, D_MODEL))
    cache_k = nrm((DEPTH, n_pool, PAGE_SIZE, A_HEADS, 2, A_DQK))
    cache_v = nrm((DEPTH, n_pool, PAGE_SIZE, A_HEADS, A_DV))
    page_table = jax.random.permutation(next(ks), n_pool)[:n_used].reshape(DEC_BATCH, n_pages).astype(jnp.int32)
    cache_mem_k = nrm((DEPTH, DEC_BATCH, N_MEM, X_HEADS, X_DH))
    cache_mem_v = nrm((DEPTH, DEC_BATCH, N_MEM, X_HEADS, X_DH))
    state_C = nrm((DEPTH, DEC_BATCH, M_HEADS, M_DH, M_DH), 0.1)
    state_n = nrm((DEPTH, DEC_BATCH, M_HEADS, M_DH), 0.1)
    state_m = nrm((DEPTH, DEC_BATCH, M_HEADS))
    state_conv = nrm((DEPTH, DEC_BATCH, CONV_W - 1, mw2))
    b_if = jnp.concatenate([nrm((DEPTH, M_HEADS), 0.1),
                            jnp.linspace(3.0, 6.0, M_HEADS, dtype=f32)[None, :] + nrm((DEPTH, M_HEADS), 0.1)], axis=-1)
    return {
        'x_prompt': x_prompt,
        'x_sample': x_sample,
        'mem_prompt': mem_prompt,
        'cache_k': cache_k,
        'cache_v': cache_v,
        'page_table': page_table,
        'cache_mem_k': cache_mem_k,
        'cache_mem_v': cache_mem_v,
        'state_C': state_C,
        'state_n': state_n,
        'state_m': state_m,
        'state_conv': state_conv,
        'g_attn_norm': gain((DEPTH, D_MODEL)),
        'w_in': nrm((DEPTH, D_MODEL, IN_COLS), D_MODEL ** -0.5),
        'b_if': b_if,
        'g_q': gain((DEPTH, A_DQK)),
        'g_k': gain((DEPTH, A_DQK)),
        'lam_q1': nrm((DEPTH, A_DQK), 0.1),
        'lam_k1': nrm((DEPTH, A_DQK), 0.1),
        'lam_q2': nrm((DEPTH, A_DQK), 0.1),
        'lam_k2': nrm((DEPTH, A_DQK), 0.1),
        'g_subln': gain((DEPTH, A_DV)),
        'w_conv': nrm((DEPTH, CONV_W, mw2), CONV_W ** -0.5),
        'b_conv': nrm((DEPTH, mw2), 0.02),
        'g_mh': gain((DEPTH, M_DH)),
        'g_mem_norm': gain((DEPTH, D_MODEL)),
        'w_mem_kv': nrm((DEPTH, D_MODEL, 2 * X_HEADS * X_DH), D_MODEL ** -0.5),
        'g_mq': gain((DEPTH, X_DH)),
        'g_mk': gain((DEPTH, X_DH)),
        'w_branch': nrm((DEPTH, N_BRANCH, BR_W, D_MODEL), BR_W ** -0.5),
        'w_o': nrm((DEPTH, D_MODEL, D_MODEL), D_MODEL ** -0.5),
        'g_ffn_norm': gain((DEPTH, D_MODEL)),
        'w_dense_gu': nrm((N_DENSE, D_MODEL, 2 * D_FF), D_MODEL ** -0.5),
        'w_dense_down': nrm((N_DENSE, D_FF, D_MODEL), D_FF ** -0.5),
        'w_router': nrm((N_MOE, D_MODEL, N_EXPERTS), D_MODEL ** -0.5),
        'b_router': nrm((N_MOE, N_EXPERTS), 0.01),
        'w_moe_gu': nrm((N_MOE, N_EXPERTS, D_MODEL, 2 * D_FF_E), D_MODEL ** -0.5),
        'w_moe_down': nrm((N_MOE, N_EXPERTS, D_FF_E, D_MODEL), D_FF_E ** -0.5),
    }


def reference(x_prompt, x_sample, mem_prompt, cache_k, cache_v, page_table, cache_mem_k, cache_mem_v,
              state_C, state_n, state_m, state_conv, g_attn_norm, w_in, b_if, g_q, g_k,
              lam_q1, lam_k1, lam_q2, lam_k2, g_subln, w_conv, b_conv, g_mh, g_mem_norm, w_mem_kv,
              g_mq, g_mk, w_branch, w_o, g_ffn_norm, w_dense_gu, w_dense_down, w_router, b_router,
              w_moe_gu, w_moe_down):
    f32 = jnp.float32
    yp, ys = x_prompt, x_sample
    bp = x_prompt.shape[0]
    names = ['kp', 'vp', 'mkp', 'mvp', 'cp', 'np', 'mp', 'convp', 'ks', 'vs', 'cs', 'ns', 'ms', 'convs']
    outs = {name: [] for name in names}
    for l in range(DEPTH):
        lam_init = 0.8 - 0.6 * math.exp(-0.3 * l)
        lam = diff_lambda(lam_q1[l], lam_k1[l], lam_q2[l], lam_k2[l], lam_init)
        p = {'w_in': w_in[l], 'b_if': b_if[l], 'g_q': g_q[l], 'g_k': g_k[l], 'g_subln': g_subln[l],
             'w_conv': w_conv[l], 'b_conv': b_conv[l], 'g_mh': g_mh[l], 'g_mq': g_mq[l],
             'w_branch': w_branch[l], 'w_o': w_o[l]}
        mk_p, mv_p = mem_kv(mem_prompt, g_mem_norm[l], w_mem_kv[l], g_mk[l])
        out_p, ka_p, va_p, conv_p, c_p, n_p, m_p = token_mixer(
            rms_norm(yp, g_attn_norm[l]), p, lam, lam_init, diff_attn_prompt, mk_p, mv_p,
            jnp.zeros((bp, CONV_W - 1, 2 * M_HEADS * M_DH), yp.dtype),
            jnp.zeros((bp, M_HEADS, M_DH, M_DH), f32), jnp.zeros((bp, M_HEADS, M_DH), f32),
            jnp.zeros((bp, M_HEADS), f32))
        yp = yp + out_p
        attn_s = functools.partial(diff_attn_sample, cache_k=cache_k, cache_v=cache_v,
                                   page_table=page_table, layer=l)
        out_s, ka_s, va_s, conv_s, c_s, n_s, m_s = token_mixer(
            rms_norm(ys, g_attn_norm[l]), p, lam, lam_init, attn_s, cache_mem_k[l], cache_mem_v[l],
            state_conv[l], state_C[l], state_n[l], state_m[l])
        ys = ys + out_s
        i = l // 2
        if l % 2 == 0:
            yp = yp + swiglu(rms_norm(yp, g_ffn_norm[l]), w_dense_gu[i], w_dense_down[i])
            ys = ys + swiglu(rms_norm(ys, g_ffn_norm[l]), w_dense_gu[i], w_dense_down[i])
        else:
            yp = yp + moe_swiglu(rms_norm(yp, g_ffn_norm[l]), w_router[i], b_router[i], w_moe_gu[i], w_moe_down[i])
            ys = ys + moe_swiglu(rms_norm(ys, g_ffn_norm[l]), w_router[i], b_router[i], w_moe_gu[i], w_moe_down[i])
        for name, val in zip(names, [ka_p, va_p, mk_p, mv_p, c_p, n_p, m_p, conv_p,
                                     ka_s, va_s, c_s, n_s, m_s, conv_s]):
            outs[name].append(val)
    st = {name: jnp.stack(outs[name], axis=0) for name in names}
    return (yp, ys, st['kp'], st['vp'], st['mkp'], st['mvp'], st['cp'], st['np'], st['mp'], st['convp'],
            st['ks'], st['vs'], st['cs'], st['ns'], st['ms'], st['convs'])
```

```python
import functools
import math

import jax
import jax.numpy as jnp
from jax import lax
from jax.experimental import pallas as pl
from jax.experimental.pallas import tpu as pltpu

F32 = jnp.float32
BF16 = jnp.bfloat16

EPS = 1e-6
NEG = -1e30

A_HEADS = 4
A_DQK = 64
A_DV = 128
M_HEADS = 4
M_DH = 128
CONV_W = 4
X_HEADS = 4
X_DH = 128
N_BRANCH = 3
N_EXPERTS = 8
LANES = 128
SUBLANES = 8

SEC = 512
C_AQ, C_AK, C_AV, C_MQ, C_MK, C_MV, C_MO, C_XQ = range(8)
Z_COLS = 8 * SEC

VMEM_LIMIT = 56 * 1024 * 1024


def _cparams(*sem):
    return pltpu.CompilerParams(dimension_semantics=sem, vmem_limit_bytes=VMEM_LIMIT)


def _dot(a, b):
    return jnp.dot(a, b, preferred_element_type=F32)


def _dot_nt(a, b):
    return lax.dot_general(a, b, (((1,), (1,)), ((), ())), preferred_element_type=F32)


def _rms_rows(x, g):
    return x * lax.rsqrt(jnp.mean(x * x, axis=-1, keepdims=True) + EPS) * g


def _log_sigmoid(x):
    return jnp.minimum(x, 0.0) - jnp.log1p(jnp.exp(-jnp.abs(x)))


def _group_norm_tile(acc, gains, group):
    outs = []
    for i in range(acc.shape[1] // LANES):
        zb = acc[:, i * LANES:(i + 1) * LANES]
        sq = zb * zb
        if group == LANES:
            ms = jnp.mean(sq, axis=-1, keepdims=True)
        else:
            lane = lax.broadcasted_iota(jnp.int32, zb.shape, 1)
            low = lane < group
            s_lo = jnp.sum(jnp.where(low, sq, 0.0), axis=-1, keepdims=True)
            s_hi = jnp.sum(jnp.where(low, 0.0, sq), axis=-1, keepdims=True)
            ms = jnp.where(low, s_lo, s_hi) * (1.0 / group)
        outs.append(zb * lax.rsqrt(ms + EPS) * gains[:, i * LANES:(i + 1) * LANES])
    return jnp.concatenate(outs, axis=1)


def _norm_proj_kernel(*refs, modes, has_extra):
    if has_extra:
        x_ref, g_ref, w_ref, gains_ref, wx_ref, o_ref, ox_ref, h_scr = refs
    else:
        x_ref, g_ref, w_ref, gains_ref, o_ref, h_scr = refs
    j = pl.program_id(1)

    @pl.when(j == 0)
    def _():
        h_scr[...] = _rms_rows(x_ref[...], g_ref[...]).astype(BF16)
        if has_extra:
            ox_ref[...] = _dot(h_scr[...], wx_ref[...])

    acc = _dot(h_scr[...], w_ref[...])
    gains = gains_ref[0]
    for mode in sorted(set(modes)):
        cond = functools.reduce(jnp.logical_or, [j == jj for jj, m in enumerate(modes) if m == mode])

        @pl.when(cond)
        def _(mode=mode):
            if mode == 0:
                o_ref[...] = acc
            else:
                o_ref[...] = _group_norm_tile(acc, gains, mode)


def norm_proj(x, g, w, gains, modes, w_extra=None, tm=512):
    t, d = x.shape
    n = w.shape[1] // SEC
    tm = min(tm, t)
    assert t % tm == 0 and len(modes) == n
    in_specs = [pl.BlockSpec((tm, d), lambda i, j: (i, 0)),
                pl.BlockSpec((1, d), lambda i, j: (0, 0)),
                pl.BlockSpec((d, SEC), lambda i, j: (0, j)),
                pl.BlockSpec((1, 1, SEC), lambda i, j: (j, 0, 0))]
    out_shape = [jax.ShapeDtypeStruct((t, n * SEC), F32)]
    out_specs = [pl.BlockSpec((tm, SEC), lambda i, j: (i, j))]
    args = [x, g, w, gains]
    if w_extra is not None:
        in_specs.append(pl.BlockSpec((d, LANES), lambda i, j: (0, 0)))
        out_shape.append(jax.ShapeDtypeStruct((t, LANES), F32))
        out_specs.append(pl.BlockSpec((tm, LANES), lambda i, j: (i, 0)))
        args.append(w_extra)
    res = pl.pallas_call(
        functools.partial(_norm_proj_kernel, modes=tuple(modes), has_extra=w_extra is not None),
        grid=(t // tm, n), in_specs=in_specs, out_specs=out_specs, out_shape=out_shape,
        scratch_shapes=[pltpu.VMEM((tm, d), BF16)],
        compiler_params=_cparams("parallel", "arbitrary"), name="norm_proj")(*args)
    return res if w_extra is not None else res[0]


def _attn_prefill_kernel(lam_ref, q_ref, k_ref, v_ref, gs_ref, o_ref, kb, vt, m_scr, l_scr, acc_scr,
                         *, tq, out_scale):
    qi = pl.program_id(2)
    nkv = kb.shape[0]

    @pl.when(qi == 0)
    def _():
        for jj in range(nkv):
            kb[jj] = k_ref[0, jj * tq:(jj + 1) * tq, :].astype(BF16)
            vt[jj] = v_ref[0, jj * tq:(jj + 1) * tq, :].T.astype(BF16)

    q = q_ref[0]
    lane = lax.broadcasted_iota(jnp.int32, q.shape, 1)
    qq = jnp.concatenate([jnp.where(lane < A_DQK, q, 0.0).astype(BF16),
                          jnp.where(lane >= A_DQK, q, 0.0).astype(BF16)], axis=0)

    m_scr[...] = jnp.full(m_scr.shape, NEG, F32)
    l_scr[...] = jnp.zeros(l_scr.shape, F32)
    acc_scr[...] = jnp.zeros(acc_scr.shape, F32)

    def tile(j, masked):
        s = _dot_nt(kb[j], qq)
        if masked:
            kpos = lax.broadcasted_iota(jnp.int32, s.shape, 0)
            qpos = lax.broadcasted_iota(jnp.int32, s.shape, 1) % tq
            s = jnp.where(kpos <= qpos, s, NEG)
        m_old = m_scr[...]
        m_new = jnp.maximum(m_old, jnp.max(s, axis=0, keepdims=True))
        alpha = jnp.exp(m_old - m_new)
        p = jnp.exp(s - m_new)
        l_scr[...] = alpha * l_scr[...] + jnp.sum(p, axis=0, keepdims=True)
        acc_scr[...] = alpha * acc_scr[...] + _dot(vt[j], p.astype(BF16))
        m_scr[...] = m_new

    tile(qi, True)

    def body(j, c):
        tile(j, False)
        return c

    lax.fori_loop(0, qi, body, 0)

    inv = 1.0 / l_scr[...]
    acc = acc_scr[...]
    o = acc[:, :tq] * inv[:, :tq] - lam_ref[0] * (acc[:, tq:] * inv[:, tq:])
    o = o * lax.rsqrt(jnp.mean(o * o, axis=0, keepdims=True) + EPS) * gs_ref[...] * out_scale
    o_ref[0] = o.T


def attn_prefill(z, lam, g_subln, out_scale, tq=256):
    b, t, _ = z.shape
    tq = min(tq, t)
    assert t % tq == 0
    nq = t // tq
    hb = SEC // LANES
    return pl.pallas_call(
        functools.partial(_attn_prefill_kernel, tq=tq, out_scale=out_scale),
        grid=(b, A_HEADS, nq),
        in_specs=[pl.BlockSpec(memory_space=pltpu.SMEM),
                  pl.BlockSpec((1, tq, LANES), lambda bi, h, qi: (bi, qi, C_AQ * hb + h)),
                  pl.BlockSpec((1, t, LANES), lambda bi, h, qi: (bi, 0, C_AK * hb + h)),
                  pl.BlockSpec((1, t, LANES), lambda bi, h, qi: (bi, 0, C_AV * hb + h)),
                  pl.BlockSpec((A_DV, 1), lambda bi, h, qi: (0, 0))],
        out_specs=pl.BlockSpec((1, tq, LANES), lambda bi, h, qi: (bi, qi, h)),
        out_shape=jax.ShapeDtypeStruct((b, t, SEC), F32),
        scratch_shapes=[pltpu.VMEM((nq, tq, LANES), BF16), pltpu.VMEM((nq, LANES, tq), BF16),
                        pltpu.VMEM((1, 2 * tq), F32), pltpu.VMEM((1, 2 * tq), F32),
                        pltpu.VMEM((A_DV, 2 * tq), F32)],
        compiler_params=_cparams("parallel", "parallel", "arbitrary"), name="attn_prefill",
    )(lam, z, z, z, g_subln.reshape(A_DV, 1))


def _attn_decode_kernel(pt_ref, lam_ref, q_ref, kn_ref, vn_ref, gs_ref, *refs, pp, out_scale):
    k_refs, v_refs = refs[:pp], refs[pp:2 * pp]
    o_ref, m_scr, l_scr, acc_scr = refs[2 * pp:]
    s_id = pl.program_id(1)
    n_rows = 2 * A_HEADS

    q = q_ref[0]
    row = lax.broadcasted_iota(jnp.int32, (n_rows, SEC), 0)
    col = lax.broadcasted_iota(jnp.int32, (n_rows, SEC), 1)
    qm = jnp.where(col // A_DQK == row, q, 0.0)
    qmb = qm.astype(BF16)

    @pl.when(s_id == 0)
    def _():
        m_scr[...] = jnp.full(m_scr.shape, NEG, F32)
        l_scr[...] = jnp.zeros(l_scr.shape, F32)
        acc_scr[...] = jnp.zeros(acc_scr.shape, F32)

    s = jnp.concatenate([_dot_nt(qmb, k_refs[i][...].astype(BF16)) for i in range(pp)], axis=1)
    m_old = m_scr[...]
    m_new = jnp.maximum(m_old, jnp.max(s, axis=1, keepdims=True))
    alpha = jnp.exp(m_old - m_new)
    p = jnp.exp(s - m_new)
    l_scr[...] = alpha * l_scr[...] + jnp.sum(p, axis=1, keepdims=True)
    pb = p.astype(BF16)
    page = s.shape[1] // pp
    pv = _dot(pb[:, :page], v_refs[0][...].astype(BF16))
    for i in range(1, pp):
        pv = pv + _dot(pb[:, i * page:(i + 1) * page], v_refs[i][...].astype(BF16))
    acc_scr[...] = alpha * acc_scr[...] + pv
    m_scr[...] = m_new

    @pl.when(s_id == pl.num_programs(1) - 1)
    def _():
        s_self = jnp.sum(qm * kn_ref[0], axis=1, keepdims=True)
        m_old = m_scr[...]
        m_fin = jnp.maximum(m_old, s_self)
        alpha = jnp.exp(m_old - m_fin)
        p_self = jnp.exp(s_self - m_fin)
        l_fin = alpha * l_scr[...] + p_self
        acc = alpha * acc_scr[...] + p_self * vn_ref[0]
        coef = jnp.where(row[:, :1] % 2 == 0, 1.0, -lam_ref[0]) / l_fin
        own = col // A_DV == row // 2
        o = jnp.sum(jnp.where(own, acc * coef, 0.0), axis=0, keepdims=True)
        outs = []
        for h in range(A_HEADS):
            oh = o[:, h * A_DV:(h + 1) * A_DV]
            outs.append(_rms_rows(oh, gs_ref[...]) * out_scale)
        o_ref[0] = jnp.concatenate(outs, axis=1)


def attn_decode(z, cache_k, cache_v, page_table, layer, lam, g_subln, out_scale, pp=8):
    b = z.shape[0]
    n_pages = page_table.shape[1]
    page = cache_k.shape[2]
    while n_pages % pp:
        pp //= 2

    def page_spec(i):
        return pl.BlockSpec((None, None, page, SEC), lambda bi, s, pt: (layer, pt[bi, s * pp + i], 0, 0))

    def row_spec(c):
        return pl.BlockSpec((1, 1, SEC), lambda bi, s, pt: (bi, 0, c))

    grid_spec = pltpu.PrefetchScalarGridSpec(
        num_scalar_prefetch=1, grid=(b, n_pages // pp),
        in_specs=[pl.BlockSpec(memory_space=pltpu.SMEM), row_spec(C_AQ), row_spec(C_AK), row_spec(C_AV),
                  pl.BlockSpec((1, A_DV), lambda bi, s, pt: (0, 0))]
                 + [page_spec(i) for i in range(pp)] + [page_spec(i) for i in range(pp)],
        out_specs=pl.BlockSpec((1, 1, SEC), lambda bi, s, pt: (bi, 0, 0)),
        scratch_shapes=[pltpu.VMEM((2 * A_HEADS, 1), F32), pltpu.VMEM((2 * A_HEADS, 1), F32),
                        pltpu.VMEM((2 * A_HEADS, SEC), F32)])
    return pl.pallas_call(
        functools.partial(_attn_decode_kernel, pp=pp, out_scale=out_scale),
        grid_spec=grid_spec, out_shape=jax.ShapeDtypeStruct((b, 1, SEC), F32),
        compiler_params=_cparams("parallel", "arbitrary"), name="attn_decode",
    )(page_table, lam, z, z, z, g_subln.reshape(1, A_DV), *([cache_k] * pp), *([cache_v] * pp))


def _mlstm_prefill_kernel(zq_ref, zk_ref, v_ref, mo_ref, gif_ref, wc_ref, bc_ref, bif_ref, gmh_ref,
                          hm_ref, c_ref, n_ref, m_ref, conv_ref, xs, ct, n_scr, m_scr, *, chunk):
    c_id = pl.program_id(1)
    last = pl.num_programs(1) - 1
    L = chunk
    hw = M_HEADS * M_DH
    pad = SUBLANES

    @pl.when(c_id == 0)
    def _():
        xs[0:pad, :] = jnp.zeros((pad, 2 * hw), F32)
        ct[...] = jnp.zeros(ct.shape, F32)
        n_scr[...] = jnp.zeros(n_scr.shape, F32)
        m_scr[...] = jnp.zeros(m_scr.shape, F32)

    xs[pad:pad + L, 0:hw] = zq_ref[0]
    xs[pad:pad + L, hw:2 * hw] = zk_ref[0]
    y = bc_ref[...]
    for j in range(CONV_W):
        off = pad - (CONV_W - 1) + j
        y = y + xs[off:off + L, :] * wc_ref[j:j + 1, :]
    cq = y * jax.nn.sigmoid(y)
    xs[pad - (CONV_W - 1):pad, :] = xs[pad + L - (CONV_W - 1):pad + L, :]

    @pl.when(c_id == last)
    def _():
        conv_ref[0] = xs[pad - (CONV_W - 1):pad, :]

    g = gif_ref[0] + bif_ref[...]
    lf = _log_sigmoid(g)
    rows = lax.broadcasted_iota(jnp.int32, (L, L), 0)
    cols = lax.broadcasted_iota(jnp.int32, (L, L), 1)
    causal = rows >= cols
    tril = jnp.where(causal, 1.0, 0.0).astype(BF16)
    hi = lf.astype(BF16)
    r1 = lf - hi.astype(F32)
    mid = r1.astype(BF16)
    lo = (r1 - mid.astype(F32)).astype(BF16)
    bcum = _dot(tril, hi) + _dot(tril, mid) + _dot(tril, lo)
    bcum_t = bcum.T
    g_t = g.T

    for h in range(M_HEADS):
        hs = slice(h * M_DH, (h + 1) * M_DH)
        b_col = bcum[:, M_HEADS + h:M_HEADS + h + 1]
        b_row = bcum_t[M_HEADS + h:M_HEADS + h + 1, :]
        li_col = g[:, h:h + 1]
        li_row = g_t[h:h + 1, :]
        m_prev = m_scr[h][:, 0:1]
        log_d = jnp.where(causal, b_col - b_row + li_row, NEG)
        log_inter = b_col + m_prev
        m_t = jnp.maximum(log_inter, jnp.max(log_d, axis=1, keepdims=True))
        d = jnp.exp(log_d - m_t)
        w_inter = jnp.exp(log_inter - m_t)
        qh = cq[:, hs]
        kh = cq[:, hw + h * M_DH:hw + (h + 1) * M_DH] * (M_DH ** -0.5)
        vh = v_ref[0][:, hs]
        qb = qh.astype(BF16)
        s = _dot_nt(qb, kh.astype(BF16)) * d
        ct_h = ct[h]
        num = w_inter * _dot(qb, ct_h.astype(BF16)) + _dot(s.astype(BF16), vh.astype(BF16))
        n_prev = n_scr[h:h + 1, :]
        den = w_inter * jnp.sum(qh * n_prev, axis=1, keepdims=True) + jnp.sum(s, axis=1, keepdims=True)
        hh = num / jnp.maximum(jnp.abs(den), jnp.exp(-m_t))
        hm_ref[0, :, hs] = jax.nn.sigmoid(mo_ref[0][:, hs]) * _rms_rows(hh, gmh_ref[...])
        m_new = m_t[L - 1:L, :]
        b_last = b_col[L - 1:L, :]
        w_c = jnp.exp(b_last + m_prev - m_new)
        w_s = jnp.exp(b_last - b_col + li_col - m_new)
        ct[h] = w_c * ct_h + _dot(kh.T.astype(BF16), (w_s * vh).astype(BF16))
        n_scr[h:h + 1, :] = w_c * n_prev + jnp.sum(w_s * kh, axis=0, keepdims=True)
        m_scr[h] = jnp.broadcast_to(m_new, (1, LANES))

    @pl.when(c_id == last)
    def _():
        for h in range(M_HEADS):
            c_ref[0, h] = ct[h].T
            m_ref[0, h:h + 1, :] = m_scr[h]
        n_ref[0] = n_scr[0:M_HEADS, :]


def mlstm_prefill(z, gif, w_conv, b_conv, bif_row, g_mh, chunk=128):
    b, t, _ = z.shape
    chunk = min(chunk, t)
    assert t % chunk == 0 and t >= CONV_W - 1
    nc = t // chunk
    hw = M_HEADS * M_DH

    def zspec(c):
        return pl.BlockSpec((1, chunk, SEC), lambda bi, ci: (bi, ci, c))

    def full(shape):
        return pl.BlockSpec(shape, lambda bi, ci: (0,) * len(shape))

    return pl.pallas_call(
        functools.partial(_mlstm_prefill_kernel, chunk=chunk),
        grid=(b, nc),
        in_specs=[zspec(C_MQ), zspec(C_MK), zspec(C_MV), zspec(C_MO),
                  pl.BlockSpec((1, chunk, LANES), lambda bi, ci: (bi, ci, 0)),
                  full((CONV_W, 2 * hw)), full((1, 2 * hw)), full((1, LANES)), full((1, M_DH))],
        out_specs=[pl.BlockSpec((1, chunk, hw), lambda bi, ci: (bi, ci, 0)),
                   pl.BlockSpec((1, M_HEADS, M_DH, M_DH), lambda bi, ci: (bi, 0, 0, 0)),
                   pl.BlockSpec((1, M_HEADS, M_DH), lambda bi, ci: (bi, 0, 0)),
                   pl.BlockSpec((1, M_HEADS, LANES), lambda bi, ci: (bi, 0, 0)),
                   pl.BlockSpec((1, CONV_W - 1, 2 * hw), lambda bi, ci: (bi, 0, 0))],
        out_shape=[jax.ShapeDtypeStruct((b, t, hw), F32),
                   jax.ShapeDtypeStruct((b, M_HEADS, M_DH, M_DH), F32),
                   jax.ShapeDtypeStruct((b, M_HEADS, M_DH), F32),
                   jax.ShapeDtypeStruct((b, M_HEADS, LANES), F32),
                   jax.ShapeDtypeStruct((b, CONV_W - 1, 2 * hw), F32)],
        scratch_shapes=[pltpu.VMEM((chunk + SUBLANES, 2 * hw), F32),
                        pltpu.VMEM((M_HEADS, M_DH, M_DH), F32),
                        pltpu.VMEM((SUBLANES, M_DH), F32),
                        pltpu.VMEM((M_HEADS, 1, LANES), F32)],
        compiler_params=_cparams("parallel", "arbitrary"), name="mlstm_prefill",
    )(z, z, z, z, gif, w_conv, b_conv, bif_row, g_mh)


def _mlstm_decode_kernel(zq_ref, zk_ref, v_ref, mo_ref, gif_ref, buf_ref, c0_ref, n0_ref, m0_ref,
                         wc_ref, bc_ref, bif_ref, gmh_ref, hm_ref, c_ref, n_ref, m_ref, conv_ref):
    hw = M_HEADS * M_DH
    u = jnp.concatenate([zq_ref[0], zk_ref[0]], axis=1)
    buf = buf_ref[0]
    y = bc_ref[...]
    for j in range(CONV_W - 1):
        y = y + buf[j:j + 1, :] * wc_ref[j:j + 1, :]
    y = y + u * wc_ref[CONV_W - 1:CONV_W, :]
    cq = y * jax.nn.sigmoid(y)
    conv_ref[0] = jnp.concatenate([buf[1:CONV_W - 1, :], u], axis=0)

    g = gif_ref[0] + bif_ref[...]
    lf_all = _log_sigmoid(g)
    eye = (lax.broadcasted_iota(jnp.int32, (M_DH, M_DH), 0)
           == lax.broadcasted_iota(jnp.int32, (M_DH, M_DH), 1))
    v_all = v_ref[0]
    mo = mo_ref[0]
    m0 = m0_ref[0]
    outs, m_out = [], []
    for h in range(M_HEADS):
        hs = slice(h * M_DH, (h + 1) * M_DH)
        li = g[:, h:h + 1]
        lf = lf_all[:, M_HEADS + h:M_HEADS + h + 1]
        m_prev = m0[:, h:h + 1]
        log_inter = lf + m_prev
        m_t = jnp.maximum(log_inter, li)
        d = jnp.exp(li - m_t)
        w_inter = jnp.exp(log_inter - m_t)
        qh = cq[:, hs]
        kh = cq[:, hw + h * M_DH:hw + (h + 1) * M_DH] * (M_DH ** -0.5)
        vh = v_all[:, hs]
        s = jnp.sum(qh * kh, axis=1, keepdims=True) * d
        c_prev = c0_ref[0, h]
        cq_col = jnp.sum(c_prev * qh, axis=1, keepdims=True)
        v_col = jnp.sum(jnp.where(eye, vh, 0.0), axis=1, keepdims=True)
        num_col = w_inter * cq_col + s * v_col
        n_prev = n0_ref[0, h:h + 1, :]
        den = w_inter * jnp.sum(n_prev * qh, axis=1, keepdims=True) + s
        h_col = num_col / jnp.maximum(jnp.abs(den), jnp.exp(-m_t))
        h_row = jnp.sum(jnp.where(eye, h_col, 0.0), axis=0, keepdims=True)
        outs.append(jax.nn.sigmoid(mo[:, hs]) * _rms_rows(h_row, gmh_ref[...]))
        c_ref[0, h] = w_inter * c_prev + d * (v_col * kh)
        n_ref[0, h:h + 1, :] = w_inter * n_prev + d * kh
        m_out.append(m_t)
    hm_ref[0] = jnp.concatenate(outs, axis=1)
    lane = lax.broadcasted_iota(jnp.int32, (1, LANES), 1)
    m_row = jnp.zeros((1, LANES), F32)
    for h in range(M_HEADS):
        m_row = jnp.where(lane == h, m_out[h], m_row)
    m_ref[0] = m_row


def mlstm_decode(z, gif, buf, c0, n0, m0, w_conv, b_conv, bif_row, g_mh):
    b = z.shape[0]
    hw = M_HEADS * M_DH

    def zspec(c):
        return pl.BlockSpec((1, 1, SEC), lambda bi: (bi, 0, c))

    def per_b(shape):
        return pl.BlockSpec((1,) + shape, lambda bi: (bi,) + (0,) * len(shape))

    def full(shape):
        return pl.BlockSpec(shape, lambda bi: (0,) * len(shape))

    return pl.pallas_call(
        _mlstm_decode_kernel, grid=(b,),
        in_specs=[zspec(C_MQ), zspec(C_MK), zspec(C_MV), zspec(C_MO), per_b((1, LANES)),
                  per_b((CONV_W - 1, 2 * hw)), per_b((M_HEADS, M_DH, M_DH)), per_b((M_HEADS, M_DH)),
                  per_b((1, LANES)),
                  full((CONV_W, 2 * hw)), full((1, 2 * hw)), full((1, LANES)), full((1, M_DH))],
        out_specs=[per_b((1, hw)), per_b((M_HEADS, M_DH, M_DH)), per_b((M_HEADS, M_DH)),
                   per_b((1, LANES)), per_b((CONV_W - 1, 2 * hw))],
        out_shape=[jax.ShapeDtypeStruct((b, 1, hw), F32),
                   jax.ShapeDtypeStruct((b, M_HEADS, M_DH, M_DH), F32),
                   jax.ShapeDtypeStruct((b, M_HEADS, M_DH), F32),
                   jax.ShapeDtypeStruct((b, 1, LANES), F32),
                   jax.ShapeDtypeStruct((b, CONV_W - 1, 2 * hw), F32)],
        compiler_params=_cparams("parallel"), name="mlstm_decode",
    )(z, z, z, z, gif, buf, c0, n0, m0, w_conv, b_conv, bif_row, g_mh)


def _cross_attn_kernel(q_ref, k_ref, v_ref, o_ref):
    q = q_ref[0]
    tq = q.shape[0]
    if tq < SUBLANES:
        q = jnp.broadcast_to(q[0:1, :], (SUBLANES, q.shape[1]))
    k = k_ref[0].astype(BF16)
    v = v_ref[0].astype(BF16)
    for h in range(X_HEADS):
        hs = slice(h * X_DH, (h + 1) * X_DH)
        s = _dot_nt(q[:, hs].astype(BF16), k[:, hs]) * (X_DH ** -0.5)
        p = jnp.exp(s - jnp.max(s, axis=-1, keepdims=True))
        p = p / jnp.sum(p, axis=-1, keepdims=True)
        o = _dot(p.astype(BF16), v[:, hs])
        o_ref[0, :, hs] = o[0:tq, :]


def cross_attn(z, mk, mv, tq=512):
    b, t, _ = z.shape
    tq = min(tq, t)
    assert t % tq == 0
    n_mem = mk.shape[1]
    return pl.pallas_call(
        _cross_attn_kernel, grid=(b, t // tq),
        in_specs=[pl.BlockSpec((1, tq, SEC), lambda bi, qi: (bi, qi, C_XQ)),
                  pl.BlockSpec((1, n_mem, SEC), lambda bi, qi: (bi, 0, 0)),
                  pl.BlockSpec((1, n_mem, SEC), lambda bi, qi: (bi, 0, 0))],
        out_specs=pl.BlockSpec((1, tq, SEC), lambda bi, qi: (bi, qi, 0)),
        out_shape=jax.ShapeDtypeStruct((b, t, SEC), F32),
        compiler_params=_cparams("parallel", "parallel"), name="cross_attn",
    )(z, mk, mv)


def _merge_kernel(x_ref, g_ref, oa_ref, hm_ref, ox_ref, wg_ref, wb_ref, wo_ref, o_ref):
    x = x_ref[...]
    d = x.shape[1]
    h = _rms_rows(x, g_ref[...]).astype(BF16)
    merged = None
    for i, br in enumerate((oa_ref, hm_ref, ox_ref)):
        gate = jax.nn.sigmoid(_dot(h, wg_ref[:, i * d:(i + 1) * d]))
        term = gate * _dot(br[...].astype(BF16), wb_ref[i])
        merged = term if merged is None else merged + term
    o_ref[...] = x + _dot(merged.astype(BF16), wo_ref[...])


def merge(x, g, oa, hm, ox, w_gate, w_branch, w_o, tm=256):
    t, d = x.shape
    tm = min(tm, t)
    assert t % tm == 0

    def rows(width):
        return pl.BlockSpec((tm, width), lambda i: (i, 0))

    def full(shape):
        return pl.BlockSpec(shape, lambda i: (0,) * len(shape))

    return pl.pallas_call(
        _merge_kernel, grid=(t // tm,),
        in_specs=[rows(d), full((1, d)), rows(SEC), rows(SEC), rows(SEC),
                  full(w_gate.shape), full(w_branch.shape), full(w_o.shape)],
        out_specs=rows(d), out_shape=jax.ShapeDtypeStruct((t, d), F32),
        compiler_params=_cparams("parallel"), name="merge",
    )(x, g, oa, hm, ox, w_gate, w_branch, w_o)


def _route_top2(h, wr_hi, wr_lo, b_router):
    h_hi = h.astype(BF16)
    h_lo = (h - h_hi.astype(F32)).astype(BF16)
    logits = _dot(h_hi, wr_hi) + (_dot(h_hi, wr_lo) + _dot(h_lo, wr_hi)) + b_router
    lane = lax.broadcasted_iota(jnp.int32, logits.shape, 1)
    logits = jnp.where(lane < N_EXPERTS, logits, -jnp.inf)
    v1 = jnp.max(logits, axis=-1, keepdims=True)
    i1 = jnp.min(jnp.where(logits == v1, lane, LANES), axis=-1, keepdims=True)
    rest = jnp.where(lane == i1, -jnp.inf, logits)
    v2 = jnp.max(rest, axis=-1, keepdims=True)
    i2 = jnp.min(jnp.where(rest == v2, lane, LANES), axis=-1, keepdims=True)
    e2 = jnp.exp(v2 - v1)
    den = 1.0 + e2
    return jnp.where(lane == i1, 1.0 / den, 0.0) + jnp.where(lane == i2, e2 / den, 0.0)


def _ffn_kernel(*refs, routed):
    if routed:
        x_ref, g_ref, wg_ref, wu_ref, wd_ref, wrh_ref, wrl_ref, br_ref, o_ref, h_scr, gate_scr = refs
    else:
        x_ref, g_ref, wg_ref, wu_ref, wd_ref, o_ref, h_scr = refs
    c = pl.program_id(1)

    @pl.when(c == 0)
    def _():
        x = x_ref[...]
        h = _rms_rows(x, g_ref[...])
        h_scr[...] = h.astype(BF16)
        o_ref[...] = x
        if routed:
            gate_scr[...] = _route_top2(h, wrh_ref[...], wrl_ref[...], br_ref[...])

    h = h_scr[...]
    gg = _dot(h, wg_ref[0])
    uu = _dot(h, wu_ref[0])
    act = (gg * jax.nn.sigmoid(gg)) * uu
    y = _dot(act.astype(BF16), wd_ref[0])
    if routed:
        lane = lax.broadcasted_iota(jnp.int32, gate_scr.shape, 1)
        y = y * jnp.sum(jnp.where(lane == c, gate_scr[...], 0.0), axis=-1, keepdims=True)
    o_ref[...] += y


def ffn(x, g, w_gate, w_up, w_down, router=None, tm=512):
    t, d = x.shape
    tm = min(tm, t)
    assert t % tm == 0
    wg_arr, wg_map = w_gate
    wu_arr, wu_map = w_up
    n_c, f, _ = w_down.shape
    in_specs = [pl.BlockSpec((tm, d), lambda i, c: (i, 0)),
                pl.BlockSpec((1, d), lambda i, c: (0, 0)),
                pl.BlockSpec((1, d, f), lambda i, c: wg_map(c)),
                pl.BlockSpec((1, d, f), lambda i, c: wu_map(c)),
                pl.BlockSpec((1, f, d), lambda i, c: (c, 0, 0))]
    args = [x, g, wg_arr, wu_arr, w_down]
    scratch = [pltpu.VMEM((tm, d), BF16)]
    if router is not None:
        in_specs += [pl.BlockSpec((d, LANES), lambda i, c: (0, 0)),
                     pl.BlockSpec((d, LANES), lambda i, c: (0, 0)),
                     pl.BlockSpec((1, LANES), lambda i, c: (0, 0))]
        args += list(router)
        scratch.append(pltpu.VMEM((tm, LANES), F32))
    return pl.pallas_call(
        functools.partial(_ffn_kernel, routed=router is not None), grid=(t // tm, n_c),
        in_specs=in_specs, out_specs=pl.BlockSpec((tm, d), lambda i, c: (i, 0)),
        out_shape=jax.ShapeDtypeStruct((t, d), F32), scratch_shapes=scratch,
        compiler_params=_cparams("parallel", "arbitrary"), name="ffn")(*args)


def _pad_lanes(a):
    return jnp.pad(a, ((0, 0),) * (a.ndim - 1) + ((0, LANES - a.shape[-1]),))


def kernel(x_prompt, x_sample, mem_prompt, cache_k, cache_v, page_table, cache_mem_k, cache_mem_v, state_C, state_n, state_m, state_conv, g_attn_norm, w_in, b_if, g_q, g_k, lam_q1, lam_k1, lam_q2, lam_k2, g_subln, w_conv, b_conv, g_mh, g_mem_norm, w_mem_kv, g_mq, g_mk, w_branch, w_o, g_ffn_norm, w_dense_gu, w_dense_down, w_router, b_router, w_moe_gu, w_moe_down):
    depth = w_in.shape[0]
    bp, tp, d = x_prompt.shape
    bs, ts, _ = x_sample.shape
    assert ts == 1, "the sample group decodes one token per sequence"
    n_mem = mem_prompt.shape[1]
    hw = M_HEADS * M_DH
    n_pool, page = cache_k.shape[1], cache_k.shape[2]
    cache_k2 = cache_k.reshape(depth, n_pool, page, SEC)
    cache_v2 = cache_v.reshape(depth, n_pool, page, SEC)

    o_aq, o_ak, o_av, o_mqk = 0, 512, 1024, 1536
    o_mv, o_mo, o_mif, o_xq, o_g = 2560, 3072, 3584, 3592, 4104

    yp = x_prompt.reshape(bp * tp, d)
    ys = x_sample.reshape(bs * ts, d)
    mem = mem_prompt.reshape(bp * n_mem, d)
    outs = {k: [] for k in ("kp", "vp", "mkp", "mvp", "cp", "np", "mp", "convp",
                            "ks", "vs", "cs", "ns", "ms", "convs")}
    ones = jnp.ones((SEC,), F32)
    for l in range(depth):
        lam_init = 0.8 - 0.6 * math.exp(-0.3 * l)
        lam = (jnp.exp(jnp.sum(lam_q1[l] * lam_k1[l])) - jnp.exp(jnp.sum(lam_q2[l] * lam_k2[l]))
               + lam_init).reshape(1).astype(F32)
        wl = w_in[l]
        w_main = jnp.concatenate([wl[:, o_aq:o_mif], wl[:, o_xq:o_g]], axis=1).astype(BF16)
        w_if = _pad_lanes(wl[:, o_mif:o_xq]).astype(BF16)
        w_gate = wl[:, o_g:].astype(BF16)
        gains = jnp.stack([jnp.tile(g_q[l], SEC // A_DQK) * (A_DQK ** -0.5), jnp.tile(g_k[l], SEC // A_DQK),
                           ones, ones, ones, ones, ones, jnp.tile(g_mq[l], SEC // X_DH)])[:, None, :]
        modes = (A_DQK, A_DQK, 0, 0, 0, 0, 0, X_DH)
        g_attn = g_attn_norm[l][None, :]
        bif_row = _pad_lanes(b_if[l][None, :])
        wc, bc, gmh = w_conv[l], b_conv[l][None, :], g_mh[l][None, :]
        wb, wo = w_branch[l].astype(BF16), w_o[l].astype(BF16)
        out_scale = 1.0 - lam_init

        kv_gains = jnp.stack([jnp.tile(g_mk[l], SEC // X_DH), ones])[:, None, :]
        mkv = norm_proj(mem, g_mem_norm[l][None, :], w_mem_kv[l].astype(BF16), kv_gains, (X_DH, 0))
        mk_p = mkv[:, :SEC].reshape(bp, n_mem, SEC)
        mv_p = mkv[:, SEC:].reshape(bp, n_mem, SEC)

        z, gif = norm_proj(yp, g_attn, w_main, gains, modes, w_extra=w_if)
        z3 = z.reshape(bp, tp, Z_COLS)
        oa = attn_prefill(z3, lam, g_subln[l], out_scale)
        hm, c_p, n_p, m_p, conv_p = mlstm_prefill(z3, gif.reshape(bp, tp, LANES), wc, bc, bif_row, gmh)
        ox = cross_attn(z3, mk_p, mv_p)
        yp = merge(yp, g_attn, oa.reshape(bp * tp, SEC), hm.reshape(bp * tp, hw), ox.reshape(bp * tp, SEC),
                   w_gate, wb, wo)
        ka_p = z3[:, :, C_AK * SEC:(C_AK + 1) * SEC].reshape(bp, tp, A_HEADS, 2, A_DQK)
        va_p = z3[:, :, C_AV * SEC:(C_AV + 1) * SEC].reshape(bp, tp, A_HEADS, A_DV)

        zs, gifs = norm_proj(ys, g_attn, w_main, gains, modes, w_extra=w_if)
        zs3 = zs.reshape(bs, 1, Z_COLS)
        oa_s = attn_decode(zs3, cache_k2, cache_v2, page_table, l, lam, g_subln[l], out_scale)
        hm_s, c_s, n_s, m_s, conv_s = mlstm_decode(
            zs3, gifs.reshape(bs, 1, LANES), state_conv[l], state_C[l], state_n[l],
            _pad_lanes(state_m[l])[:, None, :], wc, bc, bif_row, gmh)
        ox_s = cross_attn(zs3, cache_mem_k[l].reshape(bs, n_mem, SEC), cache_mem_v[l].reshape(bs, n_mem, SEC))
        ys = merge(ys, g_attn, oa_s.reshape(bs, SEC), hm_s.reshape(bs, hw), ox_s.reshape(bs, SEC),
                   w_gate, wb, wo)
        ka_s = zs3[:, :, C_AK * SEC:(C_AK + 1) * SEC].reshape(bs, 1, A_HEADS, 2, A_DQK)
        va_s = zs3[:, :, C_AV * SEC:(C_AV + 1) * SEC].reshape(bs, 1, A_HEADS, A_DV)

        i = l // 2
        g_ffn = g_ffn_norm[l][None, :]
        if l % 2 == 0:
            wgu = w_dense_gu[i].astype(BF16)[None]
            d_ff = w_dense_down.shape[1]
            n_c = 2 if d_ff % (2 * LANES) == 0 else 1
            f = d_ff // n_c
            w_g = (wgu, lambda c: (0, 0, c))
            w_u = (wgu, lambda c, n_c=n_c: (0, 0, n_c + c))
            w_d = w_dense_down[i].astype(BF16).reshape(n_c, f, d)
            yp = ffn(yp, g_ffn, w_g, w_u, w_d)
            ys = ffn(ys, g_ffn, w_g, w_u, w_d)
        else:
            wgu = w_moe_gu[i].astype(BF16)
            w_g = (wgu, lambda c: (c, 0, 0))
            w_u = (wgu, lambda c: (c, 0, 1))
            w_d = w_moe_down[i].astype(BF16)
            wr = _pad_lanes(w_router[i])
            wr_hi = wr.astype(BF16)
            wr_lo = (wr - wr_hi.astype(F32)).astype(BF16)
            router = (wr_hi, wr_lo, _pad_lanes(b_router[i][None, :]))
            yp = ffn(yp, g_ffn, w_g, w_u, w_d, router)
            ys = ffn(ys, g_ffn, w_g, w_u, w_d, router)

        for name, val in zip(outs, (ka_p, va_p, mk_p.reshape(bp, n_mem, X_HEADS, X_DH),
                                    mv_p.reshape(bp, n_mem, X_HEADS, X_DH), c_p, n_p, m_p[:, :, 0], conv_p,
                                    ka_s, va_s, c_s, n_s, m_s[:, 0, :M_HEADS], conv_s)):
            outs[name].append(val)

    st = {name: jnp.stack(vals, axis=0) for name, vals in outs.items()}
    return (yp.reshape(bp, tp, d), ys.reshape(bs, ts, d), st["kp"], st["vp"], st["mkp"], st["mvp"],
            st["cp"], st["np"], st["mp"], st["convp"], st["ks"], st["vs"], st["cs"], st["ns"], st["ms"],
            st["convs"])
```

```python
import functools
import math

import jax
import jax.numpy as jnp
from jax import lax
from jax.experimental import pallas as pl
from jax.experimental.pallas import tpu as pltpu

F32 = jnp.float32
BF16 = jnp.bfloat16

EPS = 1e-6
NEG = -1e30

A_HEADS = 4
A_DQK = 64
A_DV = 128
M_HEADS = 4
M_DH = 128
CONV_W = 4
X_HEADS = 4
X_DH = 128
N_BRANCH = 3
N_EXPERTS = 8
LANES = 128
SUBLANES = 8

SEC = 512
C_AQ, C_AK, C_AV, C_MQ, C_MK, C_MV, C_MO, C_XQ = range(8)
Z_COLS = 8 * SEC

VMEM_LIMIT = 56 * 1024 * 1024


def _cparams(*sem):
    return pltpu.CompilerParams(dimension_semantics=sem, vmem_limit_bytes=VMEM_LIMIT)


def _dot(a, b):
    return jnp.dot(a, b, preferred_element_type=F32)


def _dot_nt(a, b):
    return lax.dot_general(a, b, (((1,), (1,)), ((), ())), preferred_element_type=F32)


def _rms_rows(x, g):
    return x * lax.rsqrt(jnp.mean(x * x, axis=-1, keepdims=True) + EPS) * g


def _log_sigmoid(x):
    return jnp.minimum(x, 0.0) - jnp.log1p(jnp.exp(-jnp.abs(x)))


def _group_norm_cols(zs, gains, group):
    if group == 0:
        return zs
    outs = []
    for i in range(zs.shape[1] // LANES):
        zb = zs[:, i * LANES:(i + 1) * LANES]
        sq = zb * zb
        if group == LANES:
            ms = jnp.mean(sq, axis=-1, keepdims=True)
        else:
            lane = lax.broadcasted_iota(jnp.int32, zb.shape, 1)
            low = lane < group
            s_lo = jnp.sum(jnp.where(low, sq, 0.0), axis=-1, keepdims=True)
            s_hi = jnp.sum(jnp.where(low, 0.0, sq), axis=-1, keepdims=True)
            ms = jnp.where(low, s_lo, s_hi) * (1.0 / group)
        outs.append(zb * lax.rsqrt(ms + EPS) * gains[:, i * LANES:(i + 1) * LANES])
    return jnp.concatenate(outs, axis=1)


def _norm_proj_kernel(*refs, modes, has_extra):
    if has_extra:
        x_ref, g_ref, w_ref, gains_ref, wx_ref, o_ref, ox_ref = refs
    else:
        x_ref, g_ref, w_ref, gains_ref, o_ref = refs
    h = _rms_rows(x_ref[...], g_ref[...]).astype(BF16)
    if has_extra:
        ox_ref[...] = _dot(h, wx_ref[...])
    for s, mode in enumerate(modes):
        cs = slice(s * SEC, (s + 1) * SEC)
        o_ref[:, cs] = _group_norm_cols(_dot(h, w_ref[:, cs]), gains_ref[:, cs], mode)


def norm_proj(x, g, w, gains, modes, w_extra=None, tm=512):
    t, d = x.shape
    n = w.shape[1] // SEC
    tm = min(tm, t)
    assert t % tm == 0 and len(modes) == n

    def full(shape):
        return pl.BlockSpec(shape, lambda i: (0,) * len(shape))

    in_specs = [pl.BlockSpec((tm, d), lambda i: (i, 0)), full((1, d)), full((d, n * SEC)), full((1, n * SEC))]
    out_shape = [jax.ShapeDtypeStruct((t, n * SEC), F32)]
    out_specs = [pl.BlockSpec((tm, n * SEC), lambda i: (i, 0))]
    args = [x, g, w, gains]
    if w_extra is not None:
        in_specs.append(full((d, LANES)))
        out_shape.append(jax.ShapeDtypeStruct((t, LANES), F32))
        out_specs.append(pl.BlockSpec((tm, LANES), lambda i: (i, 0)))
        args.append(w_extra)
    res = pl.pallas_call(
        functools.partial(_norm_proj_kernel, modes=tuple(modes), has_extra=w_extra is not None),
        grid=(t // tm,), in_specs=in_specs, out_specs=out_specs, out_shape=out_shape,
        compiler_params=_cparams("parallel"), name="norm_proj")(*args)
    return res if w_extra is not None else res[0]


def _attn_prefill_kernel(lam_ref, q_ref, k_ref, v_ref, gs_ref, o_ref, kb, vt, s_scr, p_scr, a_scr,
                         m_scr, l_scr, acc_scr, *, tq, out_scale):
    qi = pl.program_id(2)
    nkv = kb.shape[0]

    @pl.when(qi == 0)
    def _():
        for jj in range(nkv):
            kb[jj] = k_ref[0, jj * tq:(jj + 1) * tq, :].astype(BF16)
            vt[jj] = v_ref[0, jj * tq:(jj + 1) * tq, :].T.astype(BF16)

    q = q_ref[0]
    lane = lax.broadcasted_iota(jnp.int32, q.shape, 1)
    qq = jnp.concatenate([jnp.where(lane < A_DQK, q, 0.0).astype(BF16),
                          jnp.where(lane >= A_DQK, q, 0.0).astype(BF16)], axis=0)

    m_scr[...] = jnp.full(m_scr.shape, NEG, F32)
    l_scr[...] = jnp.zeros(l_scr.shape, F32)
    acc_scr[...] = jnp.zeros(acc_scr.shape, F32)
    p_scr[1] = jnp.zeros(p_scr.shape[1:], BF16)
    a_scr[1] = jnp.ones(a_scr.shape[1:], F32)

    def scores(j):
        return _dot_nt(kb[j], qq)

    def softmax_stage(slot):
        s = s_scr[slot]
        m_old = m_scr[...]
        m_new = jnp.maximum(m_old, jnp.max(s, axis=0, keepdims=True))
        alpha = jnp.exp2(m_old - m_new)
        p = jnp.exp2(s - m_new)
        l_scr[...] = alpha * l_scr[...] + jnp.sum(p, axis=0, keepdims=True)
        m_scr[...] = m_new
        p_scr[slot] = p.astype(BF16)
        a_scr[slot] = alpha

    def accumulate_stage(slot, jv):
        acc_scr[...] = a_scr[slot] * acc_scr[...] + _dot(vt[jv], p_scr[slot])

    def kv_of_visit(k):
        return jnp.where(k <= 0, qi, k - 1)

    s = scores(qi)
    kpos = lax.broadcasted_iota(jnp.int32, s.shape, 0)
    qpos = lax.broadcasted_iota(jnp.int32, s.shape, 1) % tq
    s_scr[0] = jnp.where(kpos <= qpos, s, NEG)

    def body(j, c):
        slot = j % 2
        accumulate_stage(1 - slot, kv_of_visit(j - 1))
        s_next = scores(j)
        softmax_stage(slot)
        s_scr[1 - slot] = s_next
        return c

    lax.fori_loop(0, qi, body, 0)
    last = qi % 2
    accumulate_stage(1 - last, kv_of_visit(qi - 1))
    softmax_stage(last)
    accumulate_stage(last, kv_of_visit(qi))

    inv = 1.0 / l_scr[...]
    acc = acc_scr[...]
    o = acc[:, :tq] * inv[:, :tq] - lam_ref[0] * (acc[:, tq:] * inv[:, tq:])
    o = o * lax.rsqrt(jnp.mean(o * o, axis=0, keepdims=True) + EPS) * gs_ref[...] * out_scale
    o_ref[0] = o.T


def attn_prefill(z, lam, g_subln, out_scale, tq=256):
    b, t, _ = z.shape
    tq = min(tq, t)
    assert t % tq == 0
    nq = t // tq
    hb = SEC // LANES
    return pl.pallas_call(
        functools.partial(_attn_prefill_kernel, tq=tq, out_scale=out_scale),
        grid=(b, A_HEADS, nq),
        in_specs=[pl.BlockSpec(memory_space=pltpu.SMEM),
                  pl.BlockSpec((1, tq, LANES), lambda bi, h, qi: (bi, qi, C_AQ * hb + h)),
                  pl.BlockSpec((1, t, LANES), lambda bi, h, qi: (bi, 0, C_AK * hb + h)),
                  pl.BlockSpec((1, t, LANES), lambda bi, h, qi: (bi, 0, C_AV * hb + h)),
                  pl.BlockSpec((A_DV, 1), lambda bi, h, qi: (0, 0))],
        out_specs=pl.BlockSpec((1, tq, LANES), lambda bi, h, qi: (bi, qi, h)),
        out_shape=jax.ShapeDtypeStruct((b, t, SEC), F32),
        scratch_shapes=[pltpu.VMEM((nq, tq, LANES), BF16), pltpu.VMEM((nq, LANES, tq), BF16),
                        pltpu.VMEM((2, tq, 2 * tq), F32), pltpu.VMEM((2, tq, 2 * tq), BF16),
                        pltpu.VMEM((2, 1, 2 * tq), F32),
                        pltpu.VMEM((1, 2 * tq), F32), pltpu.VMEM((1, 2 * tq), F32),
                        pltpu.VMEM((A_DV, 2 * tq), F32)],
        compiler_params=_cparams("parallel", "parallel", "arbitrary"), name="attn_prefill",
    )(lam, z, z, z, g_subln.reshape(A_DV, 1))


def _attn_decode_kernel(pt_ref, lam_ref, q_ref, kn_ref, vn_ref, gs_ref, *refs, pp, out_scale):
    k_refs, v_refs = refs[:pp], refs[pp:2 * pp]
    o_ref, qcol_scr, m_scr, l_scr, acc_scr = refs[2 * pp:]
    s_id = pl.program_id(1)
    n_rows = 2 * A_HEADS
    page = k_refs[0].shape[1]
    row1 = lax.broadcasted_iota(jnp.int32, (n_rows, 1), 0)

    @pl.when(s_id == 0)
    def _():
        eye = (lax.broadcasted_iota(jnp.int32, (LANES, LANES), 0)
               == lax.broadcasted_iota(jnp.int32, (LANES, LANES), 1))
        for blk in range(SEC // LANES):
            qb = q_ref[0][:, blk * LANES:(blk + 1) * LANES]
            q_col = jnp.sum(jnp.where(eye, qb, 0.0), axis=1, keepdims=True)
            qcol_scr[blk * LANES:(blk + 1) * LANES, :] = jnp.broadcast_to(q_col, (LANES, page))
        m_scr[...] = jnp.full(m_scr.shape, NEG, F32)
        l_scr[...] = jnp.zeros(l_scr.shape, F32)
        acc_scr[...] = jnp.zeros(acc_scr.shape, F32)

    qcol = qcol_scr[...]
    s_pages = []
    for i in range(pp):
        prod = k_refs[i][...] * qcol
        s_pages.append(jnp.concatenate(
            [jnp.sum(prod[r * A_DQK:(r + 1) * A_DQK, :], axis=0, keepdims=True) for r in range(n_rows)], axis=0))
    s = jnp.concatenate(s_pages, axis=1)
    m_old = m_scr[...]
    m_new = jnp.maximum(m_old, jnp.max(s, axis=1, keepdims=True))
    alpha = jnp.exp2(m_old - m_new)
    p = jnp.exp2(s - m_new)
    l_scr[...] = alpha * l_scr[...] + jnp.sum(p, axis=1, keepdims=True)
    pb = p.astype(BF16)
    pv = jnp.zeros((n_rows, A_DV), F32)
    for i in range(pp):
        p_i = pb[:, i * page:(i + 1) * page]
        for h in range(A_HEADS):
            v_h = v_refs[i][pl.ds(h, page, stride=A_HEADS), :]
            pv = pv + jnp.where(row1 // 2 == h, _dot(p_i, v_h.astype(BF16)), 0.0)
    acc_scr[...] = alpha * acc_scr[...] + pv
    m_scr[...] = m_new

    @pl.when(s_id == pl.num_programs(1) - 1)
    def _():
        row = lax.broadcasted_iota(jnp.int32, (n_rows, SEC), 0)
        col = lax.broadcasted_iota(jnp.int32, (n_rows, SEC), 1)
        qm = jnp.where(col // A_DQK == row, q_ref[0], 0.0)
        s_self = jnp.sum(qm * kn_ref[0], axis=1, keepdims=True)
        vn = vn_ref[0]
        vn_rows = jnp.concatenate([vn[:, (r // 2) * A_DV:(r // 2 + 1) * A_DV] for r in range(n_rows)], axis=0)
        m_old = m_scr[...]
        m_fin = jnp.maximum(m_old, s_self)
        alpha = jnp.exp2(m_old - m_fin)
        p_self = jnp.exp2(s_self - m_fin)
        l_fin = alpha * l_scr[...] + p_self
        acc = alpha * acc_scr[...] + p_self * vn_rows
        t = acc * (jnp.where(row1 % 2 == 0, 1.0, -lam_ref[0]) / l_fin)
        outs = []
        for h in range(A_HEADS):
            oh = t[2 * h:2 * h + 1, :] + t[2 * h + 1:2 * h + 2, :]
            outs.append(_rms_rows(oh, gs_ref[...]) * out_scale)
        o_ref[0] = jnp.concatenate(outs, axis=1)


def attn_decode(z, cache_kt, cache_v, page_table, layer, lam, g_subln, out_scale, pp=8):
    b = z.shape[0]
    n_pages = page_table.shape[1]
    page = cache_kt.shape[3]
    while n_pages % pp:
        pp //= 2

    def page_spec(i, shape):
        return pl.BlockSpec((None, None) + shape, lambda bi, s, pt: (layer, pt[bi, s * pp + i], 0, 0))

    def row_spec(c):
        return pl.BlockSpec((1, 1, SEC), lambda bi, s, pt: (bi, 0, c))

    grid_spec = pltpu.PrefetchScalarGridSpec(
        num_scalar_prefetch=1, grid=(b, n_pages // pp),
        in_specs=[pl.BlockSpec(memory_space=pltpu.SMEM), row_spec(C_AQ), row_spec(C_AK), row_spec(C_AV),
                  pl.BlockSpec((1, A_DV), lambda bi, s, pt: (0, 0))]
                 + [page_spec(i, (SEC, page)) for i in range(pp)]
                 + [page_spec(i, (page * A_HEADS, A_DV)) for i in range(pp)],
        out_specs=pl.BlockSpec((1, 1, SEC), lambda bi, s, pt: (bi, 0, 0)),
        scratch_shapes=[pltpu.VMEM((SEC, page), F32),
                        pltpu.VMEM((2 * A_HEADS, 1), F32), pltpu.VMEM((2 * A_HEADS, 1), F32),
                        pltpu.VMEM((2 * A_HEADS, A_DV), F32)])
    return pl.pallas_call(
        functools.partial(_attn_decode_kernel, pp=pp, out_scale=out_scale),
        grid_spec=grid_spec, out_shape=jax.ShapeDtypeStruct((b, 1, SEC), F32),
        compiler_params=_cparams("parallel", "arbitrary"), name="attn_decode",
    )(page_table, lam, z, z, z, g_subln.reshape(1, A_DV), *([cache_kt] * pp), *([cache_v] * pp))


def _mlstm_prefill_kernel(zq_ref, zk_ref, v_ref, mo_ref, gif_ref, wc_ref, bc_ref, bif_ref, gmh_ref,
                          hm_ref, c_ref, n_ref, m_ref, conv_ref, xs, ct, n_scr, m_scr, *, chunk):
    c_id = pl.program_id(1)
    last = pl.num_programs(1) - 1
    L = chunk
    hw = M_HEADS * M_DH
    pad = SUBLANES

    @pl.when(c_id == 0)
    def _():
        xs[0:pad, :] = jnp.zeros((pad, 2 * hw), F32)
        ct[...] = jnp.zeros(ct.shape, F32)
        n_scr[...] = jnp.zeros(n_scr.shape, F32)
        m_scr[...] = jnp.zeros(m_scr.shape, F32)

    xs[pad:pad + L, 0:hw] = zq_ref[0]
    xs[pad:pad + L, hw:2 * hw] = zk_ref[0]
    y = bc_ref[...]
    for j in range(CONV_W):
        off = pad - (CONV_W - 1) + j
        y = y + xs[off:off + L, :] * wc_ref[j:j + 1, :]
    cq = y * jax.nn.sigmoid(y)
    xs[pad - (CONV_W - 1):pad, :] = xs[pad + L - (CONV_W - 1):pad + L, :]

    @pl.when(c_id == last)
    def _():
        conv_ref[0] = xs[pad - (CONV_W - 1):pad, :]

    g = gif_ref[0] + bif_ref[...]
    lf = _log_sigmoid(g)
    rows = lax.broadcasted_iota(jnp.int32, (L, L), 0)
    cols = lax.broadcasted_iota(jnp.int32, (L, L), 1)
    causal = rows >= cols
    tril = jnp.where(causal, 1.0, 0.0).astype(BF16)
    hi = lf.astype(BF16)
    r1 = lf - hi.astype(F32)
    mid = r1.astype(BF16)
    lo = (r1 - mid.astype(F32)).astype(BF16)
    bcum = _dot(tril, hi) + _dot(tril, mid) + _dot(tril, lo)
    bcum_t = bcum.T
    g_t = g.T

    for h in range(M_HEADS):
        hs = slice(h * M_DH, (h + 1) * M_DH)
        b_col = bcum[:, M_HEADS + h:M_HEADS + h + 1]
        b_row = bcum_t[M_HEADS + h:M_HEADS + h + 1, :]
        li_col = g[:, h:h + 1]
        li_row = g_t[h:h + 1, :]
        m_prev = m_scr[h][:, 0:1]
        log_d = jnp.where(causal, b_col - b_row + li_row, NEG)
        log_inter = b_col + m_prev
        m_t = jnp.maximum(log_inter, jnp.max(log_d, axis=1, keepdims=True))
        d = jnp.exp(log_d - m_t)
        w_inter = jnp.exp(log_inter - m_t)
        qh = cq[:, hs]
        kh = cq[:, hw + h * M_DH:hw + (h + 1) * M_DH] * (M_DH ** -0.5)
        vh = v_ref[0][:, hs]
        qb = qh.astype(BF16)
        s = _dot_nt(qb, kh.astype(BF16)) * d
        ct_h = ct[h]
        num = w_inter * _dot(qb, ct_h.astype(BF16)) + _dot(s.astype(BF16), vh.astype(BF16))
        n_prev = n_scr[h:h + 1, :]
        den = w_inter * jnp.sum(qh * n_prev, axis=1, keepdims=True) + jnp.sum(s, axis=1, keepdims=True)
        hh = num / jnp.maximum(jnp.abs(den), jnp.exp(-m_t))
        hm_ref[0, :, hs] = jax.nn.sigmoid(mo_ref[0][:, hs]) * _rms_rows(hh, gmh_ref[...])
        m_new = m_t[L - 1:L, :]
        b_last = b_col[L - 1:L, :]
        w_c = jnp.exp(b_last + m_prev - m_new)
        w_s = jnp.exp(b_last - b_col + li_col - m_new)
        ct[h] = w_c * ct_h + _dot(kh.T.astype(BF16), (w_s * vh).astype(BF16))
        n_scr[h:h + 1, :] = w_c * n_prev + jnp.sum(w_s * kh, axis=0, keepdims=True)
        m_scr[h] = jnp.broadcast_to(m_new, (1, LANES))

    @pl.when(c_id == last)
    def _():
        for h in range(M_HEADS):
            c_ref[0, h] = ct[h].T
            m_ref[0, h:h + 1, :] = m_scr[h]
        n_ref[0] = n_scr[0:M_HEADS, :]


def mlstm_prefill(z, gif, w_conv, b_conv, bif_row, g_mh, chunk=128):
    b, t, _ = z.shape
    chunk = min(chunk, t)
    assert t % chunk == 0 and t >= CONV_W - 1
    nc = t // chunk
    hw = M_HEADS * M_DH

    def zspec(c):
        return pl.BlockSpec((1, chunk, SEC), lambda bi, ci: (bi, ci, c))

    def full(shape):
        return pl.BlockSpec(shape, lambda bi, ci: (0,) * len(shape))

    return pl.pallas_call(
        functools.partial(_mlstm_prefill_kernel, chunk=chunk),
        grid=(b, nc),
        in_specs=[zspec(C_MQ), zspec(C_MK), zspec(C_MV), zspec(C_MO),
                  pl.BlockSpec((1, chunk, LANES), lambda bi, ci: (bi, ci, 0)),
                  full((CONV_W, 2 * hw)), full((1, 2 * hw)), full((1, LANES)), full((1, M_DH))],
        out_specs=[pl.BlockSpec((1, chunk, hw), lambda bi, ci: (bi, ci, 0)),
                   pl.BlockSpec((1, M_HEADS, M_DH, M_DH), lambda bi, ci: (bi, 0, 0, 0)),
                   pl.BlockSpec((1, M_HEADS, M_DH), lambda bi, ci: (bi, 0, 0)),
                   pl.BlockSpec((1, M_HEADS, LANES), lambda bi, ci: (bi, 0, 0)),
                   pl.BlockSpec((1, CONV_W - 1, 2 * hw), lambda bi, ci: (bi, 0, 0))],
        out_shape=[jax.ShapeDtypeStruct((b, t, hw), F32),
                   jax.ShapeDtypeStruct((b, M_HEADS, M_DH, M_DH), F32),
                   jax.ShapeDtypeStruct((b, M_HEADS, M_DH), F32),
                   jax.ShapeDtypeStruct((b, M_HEADS, LANES), F32),
                   jax.ShapeDtypeStruct((b, CONV_W - 1, 2 * hw), F32)],
        scratch_shapes=[pltpu.VMEM((chunk + SUBLANES, 2 * hw), F32),
                        pltpu.VMEM((M_HEADS, M_DH, M_DH), F32),
                        pltpu.VMEM((SUBLANES, M_DH), F32),
                        pltpu.VMEM((M_HEADS, 1, LANES), F32)],
        compiler_params=_cparams("parallel", "arbitrary"), name="mlstm_prefill",
    )(z, z, z, z, gif, w_conv, b_conv, bif_row, g_mh)


def _mlstm_decode_kernel(zq_ref, zk_ref, v_ref, mo_ref, gif_ref, buf_ref, c0_ref, n0_ref, m0_ref,
                         wc_ref, bc_ref, bif_ref, gmh_ref, hm_ref, c_ref, n_ref, m_ref, conv_ref):
    hw = M_HEADS * M_DH
    u = jnp.concatenate([zq_ref[0], zk_ref[0]], axis=1)
    buf = buf_ref[0]
    y = bc_ref[...]
    for j in range(CONV_W - 1):
        y = y + buf[j:j + 1, :] * wc_ref[j:j + 1, :]
    y = y + u * wc_ref[CONV_W - 1:CONV_W, :]
    cq = y * jax.nn.sigmoid(y)
    conv_ref[0] = jnp.concatenate([buf[1:CONV_W - 1, :], u], axis=0)

    g = gif_ref[0] + bif_ref[...]
    lf_all = _log_sigmoid(g)
    eye = (lax.broadcasted_iota(jnp.int32, (M_DH, M_DH), 0)
           == lax.broadcasted_iota(jnp.int32, (M_DH, M_DH), 1))
    v_all = v_ref[0]
    mo = mo_ref[0]
    m0 = m0_ref[0]
    outs, m_out = [], []
    for h in range(M_HEADS):
        hs = slice(h * M_DH, (h + 1) * M_DH)
        li = g[:, h:h + 1]
        lf = lf_all[:, M_HEADS + h:M_HEADS + h + 1]
        m_prev = m0[:, h:h + 1]
        log_inter = lf + m_prev
        m_t = jnp.maximum(log_inter, li)
        d = jnp.exp(li - m_t)
        w_inter = jnp.exp(log_inter - m_t)
        qh = cq[:, hs]
        kh = cq[:, hw + h * M_DH:hw + (h + 1) * M_DH] * (M_DH ** -0.5)
        vh = v_all[:, hs]
        s = jnp.sum(qh * kh, axis=1, keepdims=True) * d
        c_prev = c0_ref[0, h]
        cq_col = jnp.sum(c_prev * qh, axis=1, keepdims=True)
        v_col = jnp.sum(jnp.where(eye, vh, 0.0), axis=1, keepdims=True)
        num_col = w_inter * cq_col + s * v_col
        n_prev = n0_ref[0, h:h + 1, :]
        den = w_inter * jnp.sum(n_prev * qh, axis=1, keepdims=True) + s
        h_col = num_col / jnp.maximum(jnp.abs(den), jnp.exp(-m_t))
        h_row = jnp.sum(jnp.where(eye, h_col, 0.0), axis=0, keepdims=True)
        outs.append(jax.nn.sigmoid(mo[:, hs]) * _rms_rows(h_row, gmh_ref[...]))
        c_ref[0, h] = w_inter * c_prev + d * (v_col * kh)
        n_ref[0, h:h + 1, :] = w_inter * n_prev + d * kh
        m_out.append(m_t)
    hm_ref[0] = jnp.concatenate(outs, axis=1)
    lane = lax.broadcasted_iota(jnp.int32, (1, LANES), 1)
    m_row = jnp.zeros((1, LANES), F32)
    for h in range(M_HEADS):
        m_row = jnp.where(lane == h, m_out[h], m_row)
    m_ref[0] = m_row


def mlstm_decode(z, gif, buf, c0, n0, m0, w_conv, b_conv, bif_row, g_mh):
    b = z.shape[0]
    hw = M_HEADS * M_DH

    def zspec(c):
        return pl.BlockSpec((1, 1, SEC), lambda bi: (bi, 0, c))

    def per_b(shape):
        return pl.BlockSpec((1,) + shape, lambda bi: (bi,) + (0,) * len(shape))

    def full(shape):
        return pl.BlockSpec(shape, lambda bi: (0,) * len(shape))

    return pl.pallas_call(
        _mlstm_decode_kernel, grid=(b,),
        in_specs=[zspec(C_MQ), zspec(C_MK), zspec(C_MV), zspec(C_MO), per_b((1, LANES)),
                  per_b((CONV_W - 1, 2 * hw)), per_b((M_HEADS, M_DH, M_DH)), per_b((M_HEADS, M_DH)),
                  per_b((1, LANES)),
                  full((CONV_W, 2 * hw)), full((1, 2 * hw)), full((1, LANES)), full((1, M_DH))],
        out_specs=[per_b((1, hw)), per_b((M_HEADS, M_DH, M_DH)), per_b((M_HEADS, M_DH)),
                   per_b((1, LANES)), per_b((CONV_W - 1, 2 * hw))],
        out_shape=[jax.ShapeDtypeStruct((b, 1, hw), F32),
                   jax.ShapeDtypeStruct((b, M_HEADS, M_DH, M_DH), F32),
                   jax.ShapeDtypeStruct((b, M_HEADS, M_DH), F32),
                   jax.ShapeDtypeStruct((b, 1, LANES), F32),
                   jax.ShapeDtypeStruct((b, CONV_W - 1, 2 * hw), F32)],
        compiler_params=_cparams("parallel"), name="mlstm_decode",
    )(z, z, z, z, gif, buf, c0, n0, m0, w_conv, b_conv, bif_row, g_mh)


def _cross_attn_kernel(q_ref, k_ref, v_ref, o_ref):
    q = q_ref[0]
    tq = q.shape[0]
    if tq < SUBLANES:
        q = jnp.broadcast_to(q[0:1, :], (SUBLANES, q.shape[1]))
    k = k_ref[0].astype(BF16)
    v = v_ref[0].astype(BF16)
    for h in range(X_HEADS):
        hs = slice(h * X_DH, (h + 1) * X_DH)
        s = _dot_nt(q[:, hs].astype(BF16), k[:, hs]) * (X_DH ** -0.5)
        p = jnp.exp(s - jnp.max(s, axis=-1, keepdims=True))
        p = p / jnp.sum(p, axis=-1, keepdims=True)
        o = _dot(p.astype(BF16), v[:, hs])
        o_ref[0, :, hs] = o[0:tq, :]


def cross_attn(z, mk, mv, tq=512):
    b, t, _ = z.shape
    tq = min(tq, t)
    assert t % tq == 0
    n_mem = mk.shape[1]
    return pl.pallas_call(
        _cross_attn_kernel, grid=(b, t // tq),
        in_specs=[pl.BlockSpec((1, tq, SEC), lambda bi, qi: (bi, qi, C_XQ)),
                  pl.BlockSpec((1, n_mem, SEC), lambda bi, qi: (bi, 0, 0)),
                  pl.BlockSpec((1, n_mem, SEC), lambda bi, qi: (bi, 0, 0))],
        out_specs=pl.BlockSpec((1, tq, SEC), lambda bi, qi: (bi, qi, 0)),
        out_shape=jax.ShapeDtypeStruct((b, t, SEC), F32),
        compiler_params=_cparams("parallel", "parallel"), name="cross_attn",
    )(z, mk, mv)


def _merge_kernel(x_ref, g_ref, oa_ref, hm_ref, ox_ref, wg_ref, wb_ref, wo_ref, o_ref):
    x = x_ref[...]
    d = x.shape[1]
    h = _rms_rows(x, g_ref[...]).astype(BF16)
    merged = None
    for i, br in enumerate((oa_ref, hm_ref, ox_ref)):
        gate = jax.nn.sigmoid(_dot(h, wg_ref[:, i * d:(i + 1) * d]))
        term = gate * _dot(br[...].astype(BF16), wb_ref[i])
        merged = term if merged is None else merged + term
    o_ref[...] = x + _dot(merged.astype(BF16), wo_ref[...])


def merge(x, g, oa, hm, ox, w_gate, w_branch, w_o, tm=256):
    t, d = x.shape
    tm = min(tm, t)
    assert t % tm == 0

    def rows(width):
        return pl.BlockSpec((tm, width), lambda i: (i, 0))

    def full(shape):
        return pl.BlockSpec(shape, lambda i: (0,) * len(shape))

    return pl.pallas_call(
        _merge_kernel, grid=(t // tm,),
        in_specs=[rows(d), full((1, d)), rows(SEC), rows(SEC), rows(SEC),
                  full(w_gate.shape), full(w_branch.shape), full(w_o.shape)],
        out_specs=rows(d), out_shape=jax.ShapeDtypeStruct((t, d), F32),
        compiler_params=_cparams("parallel"), name="merge",
    )(x, g, oa, hm, ox, w_gate, w_branch, w_o)


def _route_top2(h, wr_hi, wr_lo, b_router):
    h_hi = h.astype(BF16)
    h_lo = (h - h_hi.astype(F32)).astype(BF16)
    logits = _dot(h_hi, wr_hi) + (_dot(h_hi, wr_lo) + _dot(h_lo, wr_hi)) + b_router
    lane = lax.broadcasted_iota(jnp.int32, logits.shape, 1)
    logits = jnp.where(lane < N_EXPERTS, logits, -jnp.inf)
    v1 = jnp.max(logits, axis=-1, keepdims=True)
    i1 = jnp.min(jnp.where(logits == v1, lane, LANES), axis=-1, keepdims=True)
    rest = jnp.where(lane == i1, -jnp.inf, logits)
    v2 = jnp.max(rest, axis=-1, keepdims=True)
    i2 = jnp.min(jnp.where(rest == v2, lane, LANES), axis=-1, keepdims=True)
    e2 = jnp.exp(v2 - v1)
    den = 1.0 + e2
    return jnp.where(lane == i1, 1.0 / den, 0.0) + jnp.where(lane == i2, e2 / den, 0.0)


def _ffn_kernel(*refs, routed):
    if routed:
        x_ref, g_ref, wg_ref, wu_ref, wd_ref, wrh_ref, wrl_ref, br_ref, o_ref, h_scr, gate_scr = refs
    else:
        x_ref, g_ref, wg_ref, wu_ref, wd_ref, o_ref, h_scr = refs
    c = pl.program_id(1)

    @pl.when(c == 0)
    def _():
        x = x_ref[...]
        h = _rms_rows(x, g_ref[...])
        h_scr[...] = h.astype(BF16)
        o_ref[...] = x
        if routed:
            gate_scr[...] = _route_top2(h, wrh_ref[...], wrl_ref[...], br_ref[...])

    h = h_scr[...]
    gg = _dot(h, wg_ref[0])
    uu = _dot(h, wu_ref[0])
    act = (gg * jax.nn.sigmoid(gg)) * uu
    y = _dot(act.astype(BF16), wd_ref[0])
    if routed:
        lane = lax.broadcasted_iota(jnp.int32, gate_scr.shape, 1)
        y = y * jnp.sum(jnp.where(lane == c, gate_scr[...], 0.0), axis=-1, keepdims=True)
    o_ref[...] += y


def ffn(x, g, w_gate, w_up, w_down, router=None, tm=512):
    t, d = x.shape
    tm = min(tm, t)
    assert t % tm == 0
    wg_arr, wg_map = w_gate
    wu_arr, wu_map = w_up
    n_c, f, _ = w_down.shape
    in_specs = [pl.BlockSpec((tm, d), lambda i, c: (i, 0)),
                pl.BlockSpec((1, d), lambda i, c: (0, 0)),
                pl.BlockSpec((1, d, f), lambda i, c: wg_map(c)),
                pl.BlockSpec((1, d, f), lambda i, c: wu_map(c)),
                pl.BlockSpec((1, f, d), lambda i, c: (c, 0, 0))]
    args = [x, g, wg_arr, wu_arr, w_down]
    scratch = [pltpu.VMEM((tm, d), BF16)]
    if router is not None:
        in_specs += [pl.BlockSpec((d, LANES), lambda i, c: (0, 0)),
                     pl.BlockSpec((d, LANES), lambda i, c: (0, 0)),
                     pl.BlockSpec((1, LANES), lambda i, c: (0, 0))]
        args += list(router)
        scratch.append(pltpu.VMEM((tm, LANES), F32))
    return pl.pallas_call(
        functools.partial(_ffn_kernel, routed=router is not None), grid=(t // tm, n_c),
        in_specs=in_specs, out_specs=pl.BlockSpec((tm, d), lambda i, c: (i, 0)),
        out_shape=jax.ShapeDtypeStruct((t, d), F32), scratch_shapes=scratch,
        compiler_params=_cparams("parallel", "arbitrary"), name="ffn")(*args)


def _pad_lanes(a):
    return jnp.pad(a, ((0, 0),) * (a.ndim - 1) + ((0, LANES - a.shape[-1]),))


def kernel(x_prompt, x_sample, mem_prompt, cache_k, cache_v, page_table, cache_mem_k, cache_mem_v, state_C, state_n, state_m, state_conv, g_attn_norm, w_in, b_if, g_q, g_k, lam_q1, lam_k1, lam_q2, lam_k2, g_subln, w_conv, b_conv, g_mh, g_mem_norm, w_mem_kv, g_mq, g_mk, w_branch, w_o, g_ffn_norm, w_dense_gu, w_dense_down, w_router, b_router, w_moe_gu, w_moe_down):
    depth = w_in.shape[0]
    bp, tp, d = x_prompt.shape
    bs, ts, _ = x_sample.shape
    assert ts == 1, "the sample group decodes one token per sequence"
    n_mem = mem_prompt.shape[1]
    hw = M_HEADS * M_DH
    n_pool, page = cache_k.shape[1], cache_k.shape[2]
    cache_kt = jnp.transpose(cache_k, (0, 1, 3, 4, 5, 2)).reshape(depth, n_pool, SEC, page)
    cache_v2 = cache_v.reshape(depth, n_pool, page * A_HEADS, A_DV)

    o_aq, o_ak, o_av, o_mqk = 0, 512, 1024, 1536
    o_mv, o_mo, o_mif, o_xq, o_g = 2560, 3072, 3584, 3592, 4104

    yp = x_prompt.reshape(bp * tp, d)
    ys = x_sample.reshape(bs * ts, d)
    mem = mem_prompt.reshape(bp * n_mem, d)
    outs = {k: [] for k in ("kp", "vp", "mkp", "mvp", "cp", "np", "mp", "convp",
                            "ks", "vs", "cs", "ns", "ms", "convs")}
    ones = jnp.ones((SEC,), F32)
    for l in range(depth):
        lam_init = 0.8 - 0.6 * math.exp(-0.3 * l)
        lam = (jnp.exp(jnp.sum(lam_q1[l] * lam_k1[l])) - jnp.exp(jnp.sum(lam_q2[l] * lam_k2[l]))
               + lam_init).reshape(1).astype(F32)
        wl = w_in[l]
        w_main = jnp.concatenate([wl[:, o_aq:o_mif], wl[:, o_xq:o_g]], axis=1).astype(BF16)
        w_if = _pad_lanes(wl[:, o_mif:o_xq]).astype(BF16)
        w_gate = wl[:, o_g:].astype(BF16)
        q_scale = (A_DQK ** -0.5) * math.log2(math.e)
        gains = jnp.concatenate([jnp.tile(g_q[l], SEC // A_DQK) * q_scale, jnp.tile(g_k[l], SEC // A_DQK),
                                 ones, ones, ones, ones, ones, jnp.tile(g_mq[l], SEC // X_DH)])[None, :]
        modes = (A_DQK, A_DQK, 0, 0, 0, 0, 0, X_DH)
        g_attn = g_attn_norm[l][None, :]
        bif_row = _pad_lanes(b_if[l][None, :])
        wc, bc, gmh = w_conv[l], b_conv[l][None, :], g_mh[l][None, :]
        wb, wo = w_branch[l].astype(BF16), w_o[l].astype(BF16)
        out_scale = 1.0 - lam_init

        kv_gains = jnp.concatenate([jnp.tile(g_mk[l], SEC // X_DH), ones])[None, :]
        mkv = norm_proj(mem, g_mem_norm[l][None, :], w_mem_kv[l].astype(BF16), kv_gains, (X_DH, 0))
        mk_p = mkv[:, :SEC].reshape(bp, n_mem, SEC)
        mv_p = mkv[:, SEC:].reshape(bp, n_mem, SEC)

        z, gif = norm_proj(yp, g_attn, w_main, gains, modes, w_extra=w_if)
        z3 = z.reshape(bp, tp, Z_COLS)
        oa = attn_prefill(z3, lam, g_subln[l], out_scale)
        hm, c_p, n_p, m_p, conv_p = mlstm_prefill(z3, gif.reshape(bp, tp, LANES), wc, bc, bif_row, gmh)
        ox = cross_attn(z3, mk_p, mv_p)
        yp = merge(yp, g_attn, oa.reshape(bp * tp, SEC), hm.reshape(bp * tp, hw), ox.reshape(bp * tp, SEC),
                   w_gate, wb, wo)
        ka_p = z3[:, :, C_AK * SEC:(C_AK + 1) * SEC].reshape(bp, tp, A_HEADS, 2, A_DQK)
        va_p = z3[:, :, C_AV * SEC:(C_AV + 1) * SEC].reshape(bp, tp, A_HEADS, A_DV)

        zs, gifs = norm_proj(ys, g_attn, w_main, gains, modes, w_extra=w_if)
        zs3 = zs.reshape(bs, 1, Z_COLS)
        oa_s = attn_decode(zs3, cache_kt, cache_v2, page_table, l, lam, g_subln[l], out_scale)
        hm_s, c_s, n_s, m_s, conv_s = mlstm_decode(
            zs3, gifs.reshape(bs, 1, LANES), state_conv[l], state_C[l], state_n[l],
            _pad_lanes(state_m[l])[:, None, :], wc, bc, bif_row, gmh)
        ox_s = cross_attn(zs3, cache_mem_k[l].reshape(bs, n_mem, SEC), cache_mem_v[l].reshape(bs, n_mem, SEC))
        ys = merge(ys, g_attn, oa_s.reshape(bs, SEC), hm_s.reshape(bs, hw), ox_s.reshape(bs, SEC),
                   w_gate, wb, wo)
        ka_s = zs3[:, :, C_AK * SEC:(C_AK + 1) * SEC].reshape(bs, 1, A_HEADS, 2, A_DQK)
        va_s = zs3[:, :, C_AV * SEC:(C_AV + 1) * SEC].reshape(bs, 1, A_HEADS, A_DV)

        i = l // 2
        g_ffn = g_ffn_norm[l][None, :]
        if l % 2 == 0:
            wgu = w_dense_gu[i].astype(BF16)[None]
            d_ff = w_dense_down.shape[1]
            n_c = 2 if d_ff % (2 * LANES) == 0 else 1
            f = d_ff // n_c
            w_g = (wgu, lambda c: (0, 0, c))
            w_u = (wgu, lambda c, n_c=n_c: (0, 0, n_c + c))
            w_d = w_dense_down[i].astype(BF16).reshape(n_c, f, d)
            yp = ffn(yp, g_ffn, w_g, w_u, w_d)
            ys = ffn(ys, g_ffn, w_g, w_u, w_d)
        else:
            wgu = w_moe_gu[i].astype(BF16)
            w_g = (wgu, lambda c: (c, 0, 0))
            w_u = (wgu, lambda c: (c, 0, 1))
            w_d = w_moe_down[i].astype(BF16)
            wr = _pad_lanes(w_router[i])
            wr_hi = wr.astype(BF16)
            wr_lo = (wr - wr_hi.astype(F32)).astype(BF16)
            router = (wr_hi, wr_lo, _pad_lanes(b_router[i][None, :]))
            yp = ffn(yp, g_ffn, w_g, w_u, w_d, router)
            ys = ffn(ys, g_ffn, w_g, w_u, w_d, router)

        for name, val in zip(outs, (ka_p, va_p, mk_p.reshape(bp, n_mem, X_HEADS, X_DH),
                                    mv_p.reshape(bp, n_mem, X_HEADS, X_DH), c_p, n_p, m_p[:, :, 0], conv_p,
                                    ka_s, va_s, c_s, n_s, m_s[:, 0, :M_HEADS], conv_s)):
            outs[name].append(val)

    st = {name: jnp.stack(vals, axis=0) for name, vals in outs.items()}
    return (yp.reshape(bp, tp, d), ys.reshape(bs, ts, d), st["kp"], st["vp"], st["mkp"], st["mvp"],
            st["cp"], st["np"], st["mp"], st["convp"], st["ks"], st["vs"], st["cs"], st["ns"], st["ms"],
            st["convs"])
```

```python
import functools
import math

import jax
import jax.numpy as jnp
from jax import lax
from jax.experimental import pallas as pl
from jax.experimental.pallas import tpu as pltpu

F32 = jnp.float32
BF16 = jnp.bfloat16

EPS = 1e-6
NEG = -1e30

A_HEADS = 4
A_DQK = 64
A_DV = 128
M_HEADS = 4
M_DH = 128
CONV_W = 4
X_HEADS = 4
X_DH = 128
N_BRANCH = 3
N_EXPERTS = 8
LANES = 128
SUBLANES = 8

SEC = 512
C_AQ, C_AK, C_AV, C_MQ, C_MK, C_MV, C_MO, C_XQ = range(8)
Z_COLS = 8 * SEC

VMEM_LIMIT = 56 * 1024 * 1024


def _cparams(*sem):
    return pltpu.CompilerParams(dimension_semantics=sem, vmem_limit_bytes=VMEM_LIMIT)


def _dot(a, b):
    return jnp.dot(a, b, preferred_element_type=F32)


def _dot_nt(a, b):
    return lax.dot_general(a, b, (((1,), (1,)), ((), ())), preferred_element_type=F32)


def _rms_rows(x, g):
    return x * lax.rsqrt(jnp.mean(x * x, axis=-1, keepdims=True) + EPS) * g


def _log_sigmoid(x):
    return jnp.minimum(x, 0.0) - jnp.log1p(jnp.exp(-jnp.abs(x)))


def _group_norm_cols(zs, gains, group):
    if group == 0:
        return zs
    outs = []
    for i in range(zs.shape[1] // LANES):
        zb = zs[:, i * LANES:(i + 1) * LANES]
        sq = zb * zb
        if group == LANES:
            ms = jnp.mean(sq, axis=-1, keepdims=True)
        else:
            lane = lax.broadcasted_iota(jnp.int32, zb.shape, 1)
            low = lane < group
            s_lo = jnp.sum(jnp.where(low, sq, 0.0), axis=-1, keepdims=True)
            s_hi = jnp.sum(jnp.where(low, 0.0, sq), axis=-1, keepdims=True)
            ms = jnp.where(low, s_lo, s_hi) * (1.0 / group)
        outs.append(zb * lax.rsqrt(ms + EPS) * gains[:, i * LANES:(i + 1) * LANES])
    return jnp.concatenate(outs, axis=1)


def _norm_proj_kernel(*refs, modes, has_extra):
    if has_extra:
        x_ref, g_ref, w_ref, gains_ref, wx_ref, o_ref, ox_ref = refs
    else:
        x_ref, g_ref, w_ref, gains_ref, o_ref = refs
    h = _rms_rows(x_ref[...], g_ref[...]).astype(BF16)
    if has_extra:
        ox_ref[...] = _dot(h, wx_ref[...])
    for s, mode in enumerate(modes):
        cs = slice(s * SEC, (s + 1) * SEC)
        o_ref[:, cs] = _group_norm_cols(_dot(h, w_ref[:, cs]), gains_ref[:, cs], mode)


def norm_proj(x, g, w, gains, modes, w_extra=None, tm=512):
    t, d = x.shape
    n = w.shape[1] // SEC
    tm = min(tm, t)
    assert t % tm == 0 and len(modes) == n

    def full(shape):
        return pl.BlockSpec(shape, lambda i: (0,) * len(shape))

    in_specs = [pl.BlockSpec((tm, d), lambda i: (i, 0)), full((1, d)), full((d, n * SEC)), full((1, n * SEC))]
    out_shape = [jax.ShapeDtypeStruct((t, n * SEC), F32)]
    out_specs = [pl.BlockSpec((tm, n * SEC), lambda i: (i, 0))]
    args = [x, g, w, gains]
    if w_extra is not None:
        in_specs.append(full((d, LANES)))
        out_shape.append(jax.ShapeDtypeStruct((t, LANES), F32))
        out_specs.append(pl.BlockSpec((tm, LANES), lambda i: (i, 0)))
        args.append(w_extra)
    res = pl.pallas_call(
        functools.partial(_norm_proj_kernel, modes=tuple(modes), has_extra=w_extra is not None),
        grid=(t // tm,), in_specs=in_specs, out_specs=out_specs, out_shape=out_shape,
        compiler_params=_cparams("parallel"), name="norm_proj")(*args)
    return res if w_extra is not None else res[0]


def _attn_prefill_kernel(lam_ref, q_ref, k_ref, v_ref, gs_ref, o_ref, kb, vt, s_scr, p_scr, a_scr,
                         m_scr, l_scr, acc_scr, *, tq, out_scale):
    qi = pl.program_id(2)
    nkv = kb.shape[0]

    @pl.when(qi == 0)
    def _():
        for jj in range(nkv):
            kb[jj] = k_ref[0, jj * tq:(jj + 1) * tq, :].astype(BF16)
            vt[jj] = v_ref[0, jj * tq:(jj + 1) * tq, :].T.astype(BF16)

    qt = q_ref[0].T
    drow = lax.broadcasted_iota(jnp.int32, qt.shape, 0)
    qq = jnp.concatenate([jnp.where(drow < A_DQK, qt, 0.0).astype(BF16),
                          jnp.where(drow >= A_DQK, qt, 0.0).astype(BF16)], axis=1)

    m_scr[...] = jnp.full(m_scr.shape, NEG, F32)
    l_scr[...] = jnp.zeros(l_scr.shape, F32)
    acc_scr[...] = jnp.zeros(acc_scr.shape, F32)
    p_scr[1] = jnp.zeros(p_scr.shape[1:], BF16)
    a_scr[1] = jnp.ones(a_scr.shape[1:], F32)

    def scores(j):
        return _dot(kb[j], qq)

    def softmax_stage(slot):
        s = s_scr[slot]
        m_old = m_scr[...]
        m_new = jnp.maximum(m_old, jnp.max(s, axis=0, keepdims=True))
        alpha = jnp.exp2(m_old - m_new)
        p = jnp.exp2(s - m_new)
        l_scr[...] = alpha * l_scr[...] + jnp.sum(p, axis=0, keepdims=True)
        m_scr[...] = m_new
        p_scr[slot] = p.astype(BF16)
        a_scr[slot] = alpha

    def accumulate_stage(slot, jv):
        acc_scr[...] = a_scr[slot] * acc_scr[...] + _dot(vt[jv], p_scr[slot])

    def kv_of_visit(k):
        return jnp.where(k <= 0, qi, k - 1)

    s = scores(qi)
    kpos = lax.broadcasted_iota(jnp.int32, s.shape, 0)
    qpos = lax.broadcasted_iota(jnp.int32, s.shape, 1) % tq
    s_scr[0] = jnp.where(kpos <= qpos, s, NEG)

    def body(j, c):
        slot = j % 2
        accumulate_stage(1 - slot, kv_of_visit(j - 1))
        s_next = scores(j)
        softmax_stage(slot)
        s_scr[1 - slot] = s_next
        return c

    lax.fori_loop(0, qi, body, 0)
    last = qi % 2
    accumulate_stage(1 - last, kv_of_visit(qi - 1))
    softmax_stage(last)
    accumulate_stage(last, kv_of_visit(qi))

    inv = 1.0 / l_scr[...]
    acc = acc_scr[...]
    o = acc[:, :tq] * inv[:, :tq] - lam_ref[0] * (acc[:, tq:] * inv[:, tq:])
    o = o * lax.rsqrt(jnp.mean(o * o, axis=0, keepdims=True) + EPS) * gs_ref[...] * out_scale
    o_ref[0] = o.T


def attn_prefill(z, lam, g_subln, out_scale, tq=256):
    b, t, _ = z.shape
    tq = min(tq, t)
    assert t % tq == 0
    nq = t // tq
    hb = SEC // LANES
    return pl.pallas_call(
        functools.partial(_attn_prefill_kernel, tq=tq, out_scale=out_scale),
        grid=(b, A_HEADS, nq),
        in_specs=[pl.BlockSpec(memory_space=pltpu.SMEM),
                  pl.BlockSpec((1, tq, LANES), lambda bi, h, qi: (bi, qi, C_AQ * hb + h)),
                  pl.BlockSpec((1, t, LANES), lambda bi, h, qi: (bi, 0, C_AK * hb + h)),
                  pl.BlockSpec((1, t, LANES), lambda bi, h, qi: (bi, 0, C_AV * hb + h)),
                  pl.BlockSpec((A_DV, 1), lambda bi, h, qi: (0, 0))],
        out_specs=pl.BlockSpec((1, tq, LANES), lambda bi, h, qi: (bi, qi, h)),
        out_shape=jax.ShapeDtypeStruct((b, t, SEC), F32),
        scratch_shapes=[pltpu.VMEM((nq, tq, LANES), BF16), pltpu.VMEM((nq, LANES, tq), BF16),
                        pltpu.VMEM((2, tq, 2 * tq), F32), pltpu.VMEM((2, tq, 2 * tq), BF16),
                        pltpu.VMEM((2, 1, 2 * tq), F32),
                        pltpu.VMEM((1, 2 * tq), F32), pltpu.VMEM((1, 2 * tq), F32),
                        pltpu.VMEM((A_DV, 2 * tq), F32)],
        compiler_params=_cparams("parallel", "parallel", "arbitrary"), name="attn_prefill",
    )(lam, z, z, z, g_subln.reshape(A_DV, 1))


def _attn_decode_kernel(pt_ref, lam_ref, q_ref, kn_ref, vn_ref, gs_ref, *refs, pp, out_scale):
    k_refs, v_refs = refs[:pp], refs[pp:2 * pp]
    o_ref, qcol_scr, p_scr, a_scr, m_scr, l_scr, acc_scr = refs[2 * pp:]
    s_id = pl.program_id(1)
    n_steps = pl.num_programs(1) - 1
    n_rows = 2 * A_HEADS
    page = k_refs[0].shape[1]
    row1 = lax.broadcasted_iota(jnp.int32, (n_rows, 1), 0)

    @pl.when(s_id == 0)
    def _():
        eye = (lax.broadcasted_iota(jnp.int32, (LANES, LANES), 0)
               == lax.broadcasted_iota(jnp.int32, (LANES, LANES), 1))
        for blk in range(SEC // LANES):
            qb = q_ref[0][:, blk * LANES:(blk + 1) * LANES]
            q_col = jnp.sum(jnp.where(eye, qb, 0.0), axis=1, keepdims=True)
            qcol_scr[blk * LANES:(blk + 1) * LANES, :] = jnp.broadcast_to(q_col, (LANES, page))
        m_scr[...] = jnp.full(m_scr.shape, NEG, F32)
        l_scr[...] = jnp.zeros(l_scr.shape, F32)
        acc_scr[...] = jnp.zeros(acc_scr.shape, F32)
        p_scr[...] = jnp.zeros(p_scr.shape, BF16)
        a_scr[...] = jnp.ones(a_scr.shape, F32)

    pb = p_scr[...]
    pv = jnp.zeros((n_rows, A_DV), F32)
    for i in range(pp):
        p_i = pb[:, i * page:(i + 1) * page]
        for h in range(A_HEADS):
            v_h = v_refs[i][pl.ds(h, page, stride=A_HEADS), :]
            pv = pv + jnp.where(row1 // 2 == h, _dot(p_i, v_h.astype(BF16)), 0.0)
    acc_scr[...] = a_scr[...] * acc_scr[...] + pv

    qcol = qcol_scr[...]
    s_pages = []
    for i in range(pp):
        prod = k_refs[i][...] * qcol
        s_pages.append(jnp.concatenate(
            [jnp.sum(prod[r * A_DQK:(r + 1) * A_DQK, :], axis=0, keepdims=True) for r in range(n_rows)], axis=0))
    s = jnp.concatenate(s_pages, axis=1)
    s = jnp.where(s_id < n_steps, s, NEG)
    m_old = m_scr[...]
    m_new = jnp.maximum(m_old, jnp.max(s, axis=1, keepdims=True))
    alpha = jnp.exp2(m_old - m_new)
    p = jnp.exp2(s - m_new)
    l_scr[...] = alpha * l_scr[...] + jnp.sum(p, axis=1, keepdims=True)
    m_scr[...] = m_new
    p_scr[...] = p.astype(BF16)
    a_scr[...] = alpha

    @pl.when(s_id == n_steps)
    def _():
        row = lax.broadcasted_iota(jnp.int32, (n_rows, SEC), 0)
        col = lax.broadcasted_iota(jnp.int32, (n_rows, SEC), 1)
        qm = jnp.where(col // A_DQK == row, q_ref[0], 0.0)
        s_self = jnp.sum(qm * kn_ref[0], axis=1, keepdims=True)
        vn = vn_ref[0]
        vn_rows = jnp.concatenate([vn[:, (r // 2) * A_DV:(r // 2 + 1) * A_DV] for r in range(n_rows)], axis=0)
        m_old = m_scr[...]
        m_fin = jnp.maximum(m_old, s_self)
        alpha = jnp.exp2(m_old - m_fin)
        p_self = jnp.exp2(s_self - m_fin)
        l_fin = alpha * l_scr[...] + p_self
        acc = alpha * acc_scr[...] + p_self * vn_rows
        t = acc * (jnp.where(row1 % 2 == 0, 1.0, -lam_ref[0]) / l_fin)
        outs = []
        for h in range(A_HEADS):
            oh = t[2 * h:2 * h + 1, :] + t[2 * h + 1:2 * h + 2, :]
            outs.append(_rms_rows(oh, gs_ref[...]) * out_scale)
        o_ref[0] = jnp.concatenate(outs, axis=1)


def attn_decode(z, cache_kt, cache_v, page_table, layer, lam, g_subln, out_scale, pp=8):
    b = z.shape[0]
    n_pages = page_table.shape[1]
    page = cache_kt.shape[3]
    while n_pages % pp:
        pp //= 2

    n_steps = n_pages // pp

    def page_spec(i, shape, lag):
        def index(bi, s, pt):
            step = jnp.clip(s - lag, 0, n_steps - 1)
            return (layer, pt[bi, step * pp + i], 0, 0)
        return pl.BlockSpec((None, None) + shape, index)

    def row_spec(c):
        return pl.BlockSpec((1, 1, SEC), lambda bi, s, pt: (bi, 0, c))

    grid_spec = pltpu.PrefetchScalarGridSpec(
        num_scalar_prefetch=1, grid=(b, n_steps + 1),
        in_specs=[pl.BlockSpec(memory_space=pltpu.SMEM), row_spec(C_AQ), row_spec(C_AK), row_spec(C_AV),
                  pl.BlockSpec((1, A_DV), lambda bi, s, pt: (0, 0))]
                 + [page_spec(i, (SEC, page), 0) for i in range(pp)]
                 + [page_spec(i, (page * A_HEADS, A_DV), 1) for i in range(pp)],
        out_specs=pl.BlockSpec((1, 1, SEC), lambda bi, s, pt: (bi, 0, 0)),
        scratch_shapes=[pltpu.VMEM((SEC, page), F32),
                        pltpu.VMEM((2 * A_HEADS, pp * page), BF16), pltpu.VMEM((2 * A_HEADS, 1), F32),
                        pltpu.VMEM((2 * A_HEADS, 1), F32), pltpu.VMEM((2 * A_HEADS, 1), F32),
                        pltpu.VMEM((2 * A_HEADS, A_DV), F32)])
    return pl.pallas_call(
        functools.partial(_attn_decode_kernel, pp=pp, out_scale=out_scale),
        grid_spec=grid_spec, out_shape=jax.ShapeDtypeStruct((b, 1, SEC), F32),
        compiler_params=_cparams("parallel", "arbitrary"), name="attn_decode",
    )(page_table, lam, z, z, z, g_subln.reshape(1, A_DV), *([cache_kt] * pp), *([cache_v] * pp))


def _mlstm_prefill_kernel(zq_ref, zk_ref, v_ref, mo_ref, gif_ref, wc_ref, bc_ref, bif_ref, gmh_ref,
                          hm_ref, c_ref, n_ref, m_ref, conv_ref, xs, ct, n_scr, m_scr, *, chunk):
    c_id = pl.program_id(1)
    last = pl.num_programs(1) - 1
    L = chunk
    hw = M_HEADS * M_DH
    pad = SUBLANES

    @pl.when(c_id == 0)
    def _():
        xs[0:pad, :] = jnp.zeros((pad, 2 * hw), F32)
        ct[...] = jnp.zeros(ct.shape, F32)
        n_scr[...] = jnp.zeros(n_scr.shape, F32)
        m_scr[...] = jnp.zeros(m_scr.shape, F32)

    xs[pad:pad + L, 0:hw] = zq_ref[0]
    xs[pad:pad + L, hw:2 * hw] = zk_ref[0]
    y = bc_ref[...]
    for j in range(CONV_W):
        off = pad - (CONV_W - 1) + j
        y = y + xs[off:off + L, :] * wc_ref[j:j + 1, :]
    cq = y * jax.nn.sigmoid(y)
    xs[pad - (CONV_W - 1):pad, :] = xs[pad + L - (CONV_W - 1):pad + L, :]

    @pl.when(c_id == last)
    def _():
        conv_ref[0] = xs[pad - (CONV_W - 1):pad, :]

    g = gif_ref[0] + bif_ref[...]
    lf = _log_sigmoid(g)
    rows = lax.broadcasted_iota(jnp.int32, (L, L), 0)
    cols = lax.broadcasted_iota(jnp.int32, (L, L), 1)
    causal = rows >= cols
    tril = jnp.where(causal, 1.0, 0.0).astype(BF16)
    hi = lf.astype(BF16)
    r1 = lf - hi.astype(F32)
    mid = r1.astype(BF16)
    lo = (r1 - mid.astype(F32)).astype(BF16)
    bcum = _dot(tril, hi) + _dot(tril, mid) + _dot(tril, lo)
    bcum_t = bcum.T
    g_t = g.T

    for h in range(M_HEADS):
        hs = slice(h * M_DH, (h + 1) * M_DH)
        b_col = bcum[:, M_HEADS + h:M_HEADS + h + 1]
        b_row = bcum_t[M_HEADS + h:M_HEADS + h + 1, :]
        li_col = g[:, h:h + 1]
        li_row = g_t[h:h + 1, :]
        m_prev = m_scr[h][:, 0:1]
        log_d = jnp.where(causal, b_col - b_row + li_row, NEG)
        log_inter = b_col + m_prev
        m_t = jnp.maximum(log_inter, jnp.max(log_d, axis=1, keepdims=True))
        d = jnp.exp(log_d - m_t)
        w_inter = jnp.exp(log_inter - m_t)
        qh = cq[:, hs]
        kh = cq[:, hw + h * M_DH:hw + (h + 1) * M_DH] * (M_DH ** -0.5)
        vh = v_ref[0][:, hs]
        qb = qh.astype(BF16)
        s = _dot_nt(qb, kh.astype(BF16)) * d
        ct_h = ct[h]
        num = w_inter * _dot(qb, ct_h.astype(BF16)) + _dot(s.astype(BF16), vh.astype(BF16))
        n_prev = n_scr[h:h + 1, :]
        den = w_inter * jnp.sum(qh * n_prev, axis=1, keepdims=True) + jnp.sum(s, axis=1, keepdims=True)
        hh = num / jnp.maximum(jnp.abs(den), jnp.exp(-m_t))
        hm_ref[0, :, hs] = jax.nn.sigmoid(mo_ref[0][:, hs]) * _rms_rows(hh, gmh_ref[...])
        m_new = m_t[L - 1:L, :]
        b_last = b_col[L - 1:L, :]
        w_c = jnp.exp(b_last + m_prev - m_new)
        w_s = jnp.exp(b_last - b_col + li_col - m_new)
        ct[h] = w_c * ct_h + _dot(kh.T.astype(BF16), (w_s * vh).astype(BF16))
        n_scr[h:h + 1, :] = w_c * n_prev + jnp.sum(w_s * kh, axis=0, keepdims=True)
        m_scr[h] = jnp.broadcast_to(m_new, (1, LANES))

    @pl.when(c_id == last)
    def _():
        for h in range(M_HEADS):
            c_ref[0, h] = ct[h].T
            m_ref[0, h:h + 1, :] = m_scr[h]
        n_ref[0] = n_scr[0:M_HEADS, :]


def mlstm_prefill(z, gif, w_conv, b_conv, bif_row, g_mh, chunk=128):
    b, t, _ = z.shape
    chunk = min(chunk, t)
    assert t % chunk == 0 and t >= CONV_W - 1
    nc = t // chunk
    hw = M_HEADS * M_DH

    def zspec(c):
        return pl.BlockSpec((1, chunk, SEC), lambda bi, ci: (bi, ci, c))

    def full(shape):
        return pl.BlockSpec(shape, lambda bi, ci: (0,) * len(shape))

    return pl.pallas_call(
        functools.partial(_mlstm_prefill_kernel, chunk=chunk),
        grid=(b, nc),
        in_specs=[zspec(C_MQ), zspec(C_MK), zspec(C_MV), zspec(C_MO),
                  pl.BlockSpec((1, chunk, LANES), lambda bi, ci: (bi, ci, 0)),
                  full((CONV_W, 2 * hw)), full((1, 2 * hw)), full((1, LANES)), full((1, M_DH))],
        out_specs=[pl.BlockSpec((1, chunk, hw), lambda bi, ci: (bi, ci, 0)),
                   pl.BlockSpec((1, M_HEADS, M_DH, M_DH), lambda bi, ci: (bi, 0, 0, 0)),
                   pl.BlockSpec((1, M_HEADS, M_DH), lambda bi, ci: (bi, 0, 0)),
                   pl.BlockSpec((1, M_HEADS, LANES), lambda bi, ci: (bi, 0, 0)),
                   pl.BlockSpec((1, CONV_W - 1, 2 * hw), lambda bi, ci: (bi, 0, 0))],
        out_shape=[jax.ShapeDtypeStruct((b, t, hw), F32),
                   jax.ShapeDtypeStruct((b, M_HEADS, M_DH, M_DH), F32),
                   jax.ShapeDtypeStruct((b, M_HEADS, M_DH), F32),
                   jax.ShapeDtypeStruct((b, M_HEADS, LANES), F32),
                   jax.ShapeDtypeStruct((b, CONV_W - 1, 2 * hw), F32)],
        scratch_shapes=[pltpu.VMEM((chunk + SUBLANES, 2 * hw), F32),
                        pltpu.VMEM((M_HEADS, M_DH, M_DH), F32),
                        pltpu.VMEM((SUBLANES, M_DH), F32),
                        pltpu.VMEM((M_HEADS, 1, LANES), F32)],
        compiler_params=_cparams("parallel", "arbitrary"), name="mlstm_prefill",
    )(z, z, z, z, gif, w_conv, b_conv, bif_row, g_mh)


def _mlstm_decode_kernel(zq_ref, zk_ref, v_ref, mo_ref, gif_ref, buf_ref, c0_ref, n0_ref, m0_ref,
                         wc_ref, bc_ref, bif_ref, gmh_ref, hm_ref, c_ref, n_ref, m_ref, conv_ref):
    hw = M_HEADS * M_DH
    u = jnp.concatenate([zq_ref[0], zk_ref[0]], axis=1)
    buf = buf_ref[0]
    y = bc_ref[...]
    for j in range(CONV_W - 1):
        y = y + buf[j:j + 1, :] * wc_ref[j:j + 1, :]
    y = y + u * wc_ref[CONV_W - 1:CONV_W, :]
    cq = y * jax.nn.sigmoid(y)
    conv_ref[0] = jnp.concatenate([buf[1:CONV_W - 1, :], u], axis=0)

    g = gif_ref[0] + bif_ref[...]
    lf_all = _log_sigmoid(g)
    eye = (lax.broadcasted_iota(jnp.int32, (M_DH, M_DH), 0)
           == lax.broadcasted_iota(jnp.int32, (M_DH, M_DH), 1))
    v_all = v_ref[0]
    mo = mo_ref[0]
    m0 = m0_ref[0]
    outs, m_out = [], []
    for h in range(M_HEADS):
        hs = slice(h * M_DH, (h + 1) * M_DH)
        li = g[:, h:h + 1]
        lf = lf_all[:, M_HEADS + h:M_HEADS + h + 1]
        m_prev = m0[:, h:h + 1]
        log_inter = lf + m_prev
        m_t = jnp.maximum(log_inter, li)
        d = jnp.exp(li - m_t)
        w_inter = jnp.exp(log_inter - m_t)
        qh = cq[:, hs]
        kh = cq[:, hw + h * M_DH:hw + (h + 1) * M_DH] * (M_DH ** -0.5)
        vh = v_all[:, hs]
        s = jnp.sum(qh * kh, axis=1, keepdims=True) * d
        c_prev = c0_ref[0, h]
        cq_col = jnp.sum(c_prev * qh, axis=1, keepdims=True)
        v_col = jnp.sum(jnp.where(eye, vh, 0.0), axis=1, keepdims=True)
        num_col = w_inter * cq_col + s * v_col
        n_prev = n0_ref[0, h:h + 1, :]
        den = w_inter * jnp.sum(n_prev * qh, axis=1, keepdims=True) + s
        h_col = num_col / jnp.maximum(jnp.abs(den), jnp.exp(-m_t))
        h_row = jnp.sum(jnp.where(eye, h_col, 0.0), axis=0, keepdims=True)
        outs.append(jax.nn.sigmoid(mo[:, hs]) * _rms_rows(h_row, gmh_ref[...]))
        c_ref[0, h] = w_inter * c_prev + d * (v_col * kh)
        n_ref[0, h:h + 1, :] = w_inter * n_prev + d * kh
        m_out.append(m_t)
    hm_ref[0] = jnp.concatenate(outs, axis=1)
    lane = lax.broadcasted_iota(jnp.int32, (1, LANES), 1)
    m_row = jnp.zeros((1, LANES), F32)
    for h in range(M_HEADS):
        m_row = jnp.where(lane == h, m_out[h], m_row)
    m_ref[0] = m_row


def mlstm_decode(z, gif, buf, c0, n0, m0, w_conv, b_conv, bif_row, g_mh):
    b = z.shape[0]
    hw = M_HEADS * M_DH

    def zspec(c):
        return pl.BlockSpec((1, 1, SEC), lambda bi: (bi, 0, c))

    def per_b(shape):
        return pl.BlockSpec((1,) + shape, lambda bi: (bi,) + (0,) * len(shape))

    def full(shape):
        return pl.BlockSpec(shape, lambda bi: (0,) * len(shape))

    return pl.pallas_call(
        _mlstm_decode_kernel, grid=(b,),
        in_specs=[zspec(C_MQ), zspec(C_MK), zspec(C_MV), zspec(C_MO), per_b((1, LANES)),
                  per_b((CONV_W - 1, 2 * hw)), per_b((M_HEADS, M_DH, M_DH)), per_b((M_HEADS, M_DH)),
                  per_b((1, LANES)),
                  full((CONV_W, 2 * hw)), full((1, 2 * hw)), full((1, LANES)), full((1, M_DH))],
        out_specs=[per_b((1, hw)), per_b((M_HEADS, M_DH, M_DH)), per_b((M_HEADS, M_DH)),
                   per_b((1, LANES)), per_b((CONV_W - 1, 2 * hw))],
        out_shape=[jax.ShapeDtypeStruct((b, 1, hw), F32),
                   jax.ShapeDtypeStruct((b, M_HEADS, M_DH, M_DH), F32),
                   jax.ShapeDtypeStruct((b, M_HEADS, M_DH), F32),
                   jax.ShapeDtypeStruct((b, 1, LANES), F32),
                   jax.ShapeDtypeStruct((b, CONV_W - 1, 2 * hw), F32)],
        compiler_params=_cparams("parallel"), name="mlstm_decode",
    )(z, z, z, z, gif, buf, c0, n0, m0, w_conv, b_conv, bif_row, g_mh)


def _cross_attn_kernel(q_ref, k_ref, v_ref, o_ref):
    q = q_ref[0]
    tq = q.shape[0]
    if tq < SUBLANES:
        q = jnp.broadcast_to(q[0:1, :], (SUBLANES, q.shape[1]))
    k = k_ref[0].astype(BF16)
    v = v_ref[0].astype(BF16)
    for h in range(X_HEADS):
        hs = slice(h * X_DH, (h + 1) * X_DH)
        s = _dot_nt(q[:, hs].astype(BF16), k[:, hs]) * (X_DH ** -0.5)
        p = jnp.exp(s - jnp.max(s, axis=-1, keepdims=True))
        p = p / jnp.sum(p, axis=-1, keepdims=True)
        o = _dot(p.astype(BF16), v[:, hs])
        o_ref[0, :, hs] = o[0:tq, :]


def cross_attn(z, mk, mv, tq=512):
    b, t, _ = z.shape
    tq = min(tq, t)
    assert t % tq == 0
    n_mem = mk.shape[1]
    return pl.pallas_call(
        _cross_attn_kernel, grid=(b, t // tq),
        in_specs=[pl.BlockSpec((1, tq, SEC), lambda bi, qi: (bi, qi, C_XQ)),
                  pl.BlockSpec((1, n_mem, SEC), lambda bi, qi: (bi, 0, 0)),
                  pl.BlockSpec((1, n_mem, SEC), lambda bi, qi: (bi, 0, 0))],
        out_specs=pl.BlockSpec((1, tq, SEC), lambda bi, qi: (bi, qi, 0)),
        out_shape=jax.ShapeDtypeStruct((b, t, SEC), F32),
        compiler_params=_cparams("parallel", "parallel"), name="cross_attn",
    )(z, mk, mv)


def _merge_kernel(x_ref, g_ref, oa_ref, hm_ref, ox_ref, wg_ref, wb_ref, wo_ref, o_ref):
    x = x_ref[...]
    d = x.shape[1]
    h = _rms_rows(x, g_ref[...]).astype(BF16)
    merged = None
    for i, br in enumerate((oa_ref, hm_ref, ox_ref)):
        gate = jax.nn.sigmoid(_dot(h, wg_ref[:, i * d:(i + 1) * d]))
        term = gate * _dot(br[...].astype(BF16), wb_ref[i])
        merged = term if merged is None else merged + term
    o_ref[...] = x + _dot(merged.astype(BF16), wo_ref[...])


def merge(x, g, oa, hm, ox, w_gate, w_branch, w_o, tm=256):
    t, d = x.shape
    tm = min(tm, t)
    assert t % tm == 0

    def rows(width):
        return pl.BlockSpec((tm, width), lambda i: (i, 0))

    def full(shape):
        return pl.BlockSpec(shape, lambda i: (0,) * len(shape))

    return pl.pallas_call(
        _merge_kernel, grid=(t // tm,),
        in_specs=[rows(d), full((1, d)), rows(SEC), rows(SEC), rows(SEC),
                  full(w_gate.shape), full(w_branch.shape), full(w_o.shape)],
        out_specs=rows(d), out_shape=jax.ShapeDtypeStruct((t, d), F32),
        compiler_params=_cparams("parallel"), name="merge",
    )(x, g, oa, hm, ox, w_gate, w_branch, w_o)


def _route_top2(h, wr_hi, wr_lo, b_router):
    h_hi = h.astype(BF16)
    h_lo = (h - h_hi.astype(F32)).astype(BF16)
    logits = _dot(h_hi, wr_hi) + (_dot(h_hi, wr_lo) + _dot(h_lo, wr_hi)) + b_router
    lane = lax.broadcasted_iota(jnp.int32, logits.shape, 1)
    logits = jnp.where(lane < N_EXPERTS, logits, -jnp.inf)
    v1 = jnp.max(logits, axis=-1, keepdims=True)
    i1 = jnp.min(jnp.where(logits == v1, lane, LANES), axis=-1, keepdims=True)
    rest = jnp.where(lane == i1, -jnp.inf, logits)
    v2 = jnp.max(rest, axis=-1, keepdims=True)
    i2 = jnp.min(jnp.where(rest == v2, lane, LANES), axis=-1, keepdims=True)
    e2 = jnp.exp(v2 - v1)
    den = 1.0 + e2
    gates = jnp.where(lane == i1, 1.0 / den, 0.0) + jnp.where(lane == i2, e2 / den, 0.0)
    return gates, jnp.logical_or(lane == i1, lane == i2)


def _swiglu(h, wg, wu, wd):
    gg = _dot(h, wg)
    uu = _dot(h, wu)
    return _dot(((gg * jax.nn.sigmoid(gg)) * uu).astype(BF16), wd)


def _ffn_kernel(*refs, routed, blk):
    if routed and blk:
        (x_ref, g_ref, wg_ref, wu_ref, wd_ref, wrh_ref, wrl_ref, br_ref, o_ref,
         h_scr, gate_scr, key_scr, keyt_scr, cnt_scr) = refs
    elif routed:
        x_ref, g_ref, wg_ref, wu_ref, wd_ref, wrh_ref, wrl_ref, br_ref, o_ref, h_scr, gate_scr = refs
    else:
        x_ref, g_ref, wg_ref, wu_ref, wd_ref, o_ref, h_scr = refs
    c = pl.program_id(1)
    tm = x_ref.shape[0]

    @pl.when(c == 0)
    def _():
        x = x_ref[...]
        h = _rms_rows(x, g_ref[...])
        h_scr[...] = h.astype(BF16)
        o_ref[...] = x
        if routed:
            gates, sel = _route_top2(h, wrh_ref[...], wrl_ref[...], br_ref[...])
            gate_scr[...] = gates
        if routed and blk:
            earlier = (lax.broadcasted_iota(jnp.int32, (tm, tm), 1)
                       < lax.broadcasted_iota(jnp.int32, (tm, tm), 0))
            sel01 = jnp.where(sel, 1.0, 0.0)
            rank = _dot(jnp.where(earlier, 1.0, 0.0).astype(BF16), sel01.astype(BF16))
            key = jnp.where(sel, rank, -1.0)
            key_scr[...] = key
            keyt_scr[...] = key.T
            cnt_scr[...] = jnp.sum(sel01, axis=0, keepdims=True)

    if not routed:
        o_ref[...] += _swiglu(h_scr[...], wg_ref[0], wu_ref[0], wd_ref[0])
        return
    lane = lax.broadcasted_iota(jnp.int32, gate_scr.shape, 1)
    gate_col = jnp.sum(jnp.where(lane == c, gate_scr[...], 0.0), axis=-1, keepdims=True)
    if not blk:
        o_ref[...] += gate_col * _swiglu(h_scr[...], wg_ref[0], wu_ref[0], wd_ref[0])
        return

    key_col = jnp.sum(jnp.where(lane == c, key_scr[...], 0.0), axis=-1, keepdims=True)
    key_row = keyt_scr[pl.ds(c, 1), :]
    lane1 = lax.broadcasted_iota(jnp.int32, cnt_scr.shape, 1)
    count = jnp.sum(jnp.where(lane1 == c, cnt_scr[...], 0.0)).astype(jnp.int32)
    slot_rows = lax.broadcasted_iota(jnp.int32, (blk, tm), 0).astype(F32)
    slot_cols = lax.broadcasted_iota(jnp.int32, (tm, blk), 1).astype(F32)
    half = tm // 2

    def block(b, carry):
        base = (b * blk).astype(F32)
        pick = jnp.where(key_row - base == slot_rows, 1.0, 0.0).astype(BF16)
        h_sel = _dot(pick, h_scr[...]).astype(BF16)
        y = _swiglu(h_sel, wg_ref[0], wu_ref[0], wd_ref[0]).astype(BF16)
        place = jnp.where(key_col - base == slot_cols, 1.0, 0.0).astype(BF16)
        for r in range(2):
            rows = slice(r * half, (r + 1) * half)
            o_ref[rows, :] += gate_col[rows, :] * _dot(place[rows, :], y)
        return carry

    lax.fori_loop(0, (count + blk - 1) // blk, block, 0)


def ffn(x, g, w_gate, w_up, w_down, router=None, tm=512, blk=256):
    t, d = x.shape
    if router is not None:
        tm = 2 * tm
    tm = min(tm, t)
    assert t % tm == 0
    blk = blk if (router is not None and tm >= 2 * blk) else 0
    wg_arr, wg_map = w_gate
    wu_arr, wu_map = w_up
    n_c, f, _ = w_down.shape
    in_specs = [pl.BlockSpec((tm, d), lambda i, c: (i, 0)),
                pl.BlockSpec((1, d), lambda i, c: (0, 0)),
                pl.BlockSpec((1, d, f), lambda i, c: wg_map(c)),
                pl.BlockSpec((1, d, f), lambda i, c: wu_map(c)),
                pl.BlockSpec((1, f, d), lambda i, c: (c, 0, 0))]
    args = [x, g, wg_arr, wu_arr, w_down]
    scratch = [pltpu.VMEM((tm, d), BF16)]
    if router is not None:
        in_specs += [pl.BlockSpec((d, LANES), lambda i, c: (0, 0)),
                     pl.BlockSpec((d, LANES), lambda i, c: (0, 0)),
                     pl.BlockSpec((1, LANES), lambda i, c: (0, 0))]
        args += list(router)
        scratch.append(pltpu.VMEM((tm, LANES), F32))
        if blk:
            scratch += [pltpu.VMEM((tm, LANES), F32), pltpu.VMEM((LANES, tm), F32), pltpu.VMEM((1, LANES), F32)]
    return pl.pallas_call(
        functools.partial(_ffn_kernel, routed=router is not None, blk=blk), grid=(t // tm, n_c),
        in_specs=in_specs, out_specs=pl.BlockSpec((tm, d), lambda i, c: (i, 0)),
        out_shape=jax.ShapeDtypeStruct((t, d), F32), scratch_shapes=scratch,
        compiler_params=_cparams("parallel", "arbitrary"), name="ffn")(*args)


def _pad_lanes(a):
    return jnp.pad(a, ((0, 0),) * (a.ndim - 1) + ((0, LANES - a.shape[-1]),))


def kernel(x_prompt, x_sample, mem_prompt, cache_k, cache_v, page_table, cache_mem_k, cache_mem_v, state_C, state_n, state_m, state_conv, g_attn_norm, w_in, b_if, g_q, g_k, lam_q1, lam_k1, lam_q2, lam_k2, g_subln, w_conv, b_conv, g_mh, g_mem_norm, w_mem_kv, g_mq, g_mk, w_branch, w_o, g_ffn_norm, w_dense_gu, w_dense_down, w_router, b_router, w_moe_gu, w_moe_down):
    depth = w_in.shape[0]
    bp, tp, d = x_prompt.shape
    bs, ts, _ = x_sample.shape
    assert ts == 1, "the sample group decodes one token per sequence"
    n_mem = mem_prompt.shape[1]
    hw = M_HEADS * M_DH
    n_pool, page = cache_k.shape[1], cache_k.shape[2]
    cache_kt = jnp.transpose(cache_k, (0, 1, 3, 4, 5, 2)).reshape(depth, n_pool, SEC, page)
    cache_v2 = cache_v.reshape(depth, n_pool, page * A_HEADS, A_DV)

    o_aq, o_ak, o_av, o_mqk = 0, 512, 1024, 1536
    o_mv, o_mo, o_mif, o_xq, o_g = 2560, 3072, 3584, 3592, 4104

    yp = x_prompt.reshape(bp * tp, d)
    ys = x_sample.reshape(bs * ts, d)
    mem = mem_prompt.reshape(bp * n_mem, d)
    outs = {k: [] for k in ("kp", "vp", "mkp", "mvp", "cp", "np", "mp", "convp",
                            "ks", "vs", "cs", "ns", "ms", "convs")}
    ones = jnp.ones((SEC,), F32)
    for l in range(depth):
        lam_init = 0.8 - 0.6 * math.exp(-0.3 * l)
        lam = (jnp.exp(jnp.sum(lam_q1[l] * lam_k1[l])) - jnp.exp(jnp.sum(lam_q2[l] * lam_k2[l]))
               + lam_init).reshape(1).astype(F32)
        wl = w_in[l]
        w_main = jnp.concatenate([wl[:, o_aq:o_mif], wl[:, o_xq:o_g]], axis=1).astype(BF16)
        w_if = _pad_lanes(wl[:, o_mif:o_xq]).astype(BF16)
        w_gate = wl[:, o_g:].astype(BF16)
        q_scale = (A_DQK ** -0.5) * math.log2(math.e)
        gains = jnp.concatenate([jnp.tile(g_q[l], SEC // A_DQK) * q_scale, jnp.tile(g_k[l], SEC // A_DQK),
                                 ones, ones, ones, ones, ones, jnp.tile(g_mq[l], SEC // X_DH)])[None, :]
        modes = (A_DQK, A_DQK, 0, 0, 0, 0, 0, X_DH)
        g_attn = g_attn_norm[l][None, :]
        bif_row = _pad_lanes(b_if[l][None, :])
        wc, bc, gmh = w_conv[l], b_conv[l][None, :], g_mh[l][None, :]
        wb, wo = w_branch[l].astype(BF16), w_o[l].astype(BF16)
        out_scale = 1.0 - lam_init

        kv_gains = jnp.concatenate([jnp.tile(g_mk[l], SEC // X_DH), ones])[None, :]
        mkv = norm_proj(mem, g_mem_norm[l][None, :], w_mem_kv[l].astype(BF16), kv_gains, (X_DH, 0))
        mk_p = mkv[:, :SEC].reshape(bp, n_mem, SEC)
        mv_p = mkv[:, SEC:].reshape(bp, n_mem, SEC)

        z, gif = norm_proj(yp, g_attn, w_main, gains, modes, w_extra=w_if)
        z3 = z.reshape(bp, tp, Z_COLS)
        oa = attn_prefill(z3, lam, g_subln[l], out_scale)
        hm, c_p, n_p, m_p, conv_p = mlstm_prefill(z3, gif.reshape(bp, tp, LANES), wc, bc, bif_row, gmh)
        ox = cross_attn(z3, mk_p, mv_p)
        yp = merge(yp, g_attn, oa.reshape(bp * tp, SEC), hm.reshape(bp * tp, hw), ox.reshape(bp * tp, SEC),
                   w_gate, wb, wo)
        ka_p = z3[:, :, C_AK * SEC:(C_AK + 1) * SEC].reshape(bp, tp, A_HEADS, 2, A_DQK)
        va_p = z3[:, :, C_AV * SEC:(C_AV + 1) * SEC].reshape(bp, tp, A_HEADS, A_DV)

        zs, gifs = norm_proj(ys, g_attn, w_main, gains, modes, w_extra=w_if)
        zs3 = zs.reshape(bs, 1, Z_COLS)
        oa_s = attn_decode(zs3, cache_kt, cache_v2, page_table, l, lam, g_subln[l], out_scale)
        hm_s, c_s, n_s, m_s, conv_s = mlstm_decode(
            zs3, gifs.reshape(bs, 1, LANES), state_conv[l], state_C[l], state_n[l],
            _pad_lanes(state_m[l])[:, None, :], wc, bc, bif_row, gmh)
        ox_s = cross_attn(zs3, cache_mem_k[l].reshape(bs, n_mem, SEC), cache_mem_v[l].reshape(bs, n_mem, SEC))
        ys = merge(ys, g_attn, oa_s.reshape(bs, SEC), hm_s.reshape(bs, hw), ox_s.reshape(bs, SEC),
                   w_gate, wb, wo)
        ka_s = zs3[:, :, C_AK * SEC:(C_AK + 1) * SEC].reshape(bs, 1, A_HEADS, 2, A_DQK)
        va_s = zs3[:, :, C_AV * SEC:(C_AV + 1) * SEC].reshape(bs, 1, A_HEADS, A_DV)

        i = l // 2
        g_ffn = g_ffn_norm[l][None, :]
        if l % 2 == 0:
            wgu = w_dense_gu[i].astype(BF16)[None]
            d_ff = w_dense_down.shape[1]
            n_c = 2 if d_ff % (2 * LANES) == 0 else 1
            f = d_ff // n_c
            w_g = (wgu, lambda c: (0, 0, c))
            w_u = (wgu, lambda c, n_c=n_c: (0, 0, n_c + c))
            w_d = w_dense_down[i].astype(BF16).reshape(n_c, f, d)
            yp = ffn(yp, g_ffn, w_g, w_u, w_d)
            ys = ffn(ys, g_ffn, w_g, w_u, w_d)
        else:
            wgu = w_moe_gu[i].astype(BF16)
            w_g = (wgu, lambda c: (c, 0, 0))
            w_u = (wgu, lambda c: (c, 0, 1))
            w_d = w_moe_down[i].astype(BF16)
            wr = _pad_lanes(w_router[i])
            wr_hi = wr.astype(BF16)
            wr_lo = (wr - wr_hi.astype(F32)).astype(BF16)
            router = (wr_hi, wr_lo, _pad_lanes(b_router[i][None, :]))
            yp = ffn(yp, g_ffn, w_g, w_u, w_d, router)
            ys = ffn(ys, g_ffn, w_g, w_u, w_d, router)

        for name, val in zip(outs, (ka_p, va_p, mk_p.reshape(bp, n_mem, X_HEADS, X_DH),
                                    mv_p.reshape(bp, n_mem, X_HEADS, X_DH), c_p, n_p, m_p[:, :, 0], conv_p,
                                    ka_s, va_s, c_s, n_s, m_s[:, 0, :M_HEADS], conv_s)):
            outs[name].append(val)

    st = {name: jnp.stack(vals, axis=0) for name, vals in outs.items()}
    return (yp.reshape(bp, tp, d), ys.reshape(bs, ts, d), st["kp"], st["vp"], st["mkp"], st["mvp"],
            st["cp"], st["np"], st["mp"], st["convp"], st["ks"], st["vs"], st["cs"], st["ns"], st["ms"],
            st["convs"])
```

```python
import functools
import math

import jax
import jax.numpy as jnp
from jax import lax
from jax.experimental import pallas as pl
from jax.experimental.pallas import tpu as pltpu

F32 = jnp.float32
BF16 = jnp.bfloat16

EPS = 1e-6
NEG = -1e30

A_HEADS = 4
A_DQK = 64
A_DV = 128
M_HEADS = 4
M_DH = 128
CONV_W = 4
X_HEADS = 4
X_DH = 128
N_BRANCH = 3
N_EXPERTS = 8
LANES = 128
SUBLANES = 8

SEC = 512
C_AQ, C_AK, C_AV, C_MQ, C_MK, C_MV, C_MO, C_XQ = range(8)
Z_COLS = 8 * SEC

VMEM_LIMIT = 56 * 1024 * 1024


def _cparams(*sem):
    return pltpu.CompilerParams(dimension_semantics=sem, vmem_limit_bytes=VMEM_LIMIT)


def _dot(a, b):
    return jnp.dot(a, b, preferred_element_type=F32)


def _dot_nt(a, b):
    return lax.dot_general(a, b, (((1,), (1,)), ((), ())), preferred_element_type=F32)


def _rms_rows(x, g):
    return x * lax.rsqrt(jnp.mean(x * x, axis=-1, keepdims=True) + EPS) * g


def _log_sigmoid(x):
    return jnp.minimum(x, 0.0) - jnp.log1p(jnp.exp(-jnp.abs(x)))


def _group_norm_cols(zs, gains, group):
    if group == 0:
        return zs
    outs = []
    for i in range(zs.shape[1] // LANES):
        zb = zs[:, i * LANES:(i + 1) * LANES]
        sq = zb * zb
        if group == LANES:
            ms = jnp.mean(sq, axis=-1, keepdims=True)
        else:
            lane = lax.broadcasted_iota(jnp.int32, zb.shape, 1)
            low = lane < group
            s_lo = jnp.sum(jnp.where(low, sq, 0.0), axis=-1, keepdims=True)
            s_hi = jnp.sum(jnp.where(low, 0.0, sq), axis=-1, keepdims=True)
            ms = jnp.where(low, s_lo, s_hi) * (1.0 / group)
        outs.append(zb * lax.rsqrt(ms + EPS) * gains[:, i * LANES:(i + 1) * LANES])
    return jnp.concatenate(outs, axis=1)


def _norm_proj_kernel(*refs, modes, has_extra, n_alias, tkv):
    x_ref, g_ref, w_ref, gains_ref = refs[:4]
    pos = 4
    if has_extra:
        wx_ref = refs[pos]
        pos += 1
    pos += n_alias
    o_ref = refs[pos]
    pos += 1
    if has_extra:
        ox_ref = refs[pos]
        pos += 1
    if tkv:
        qt_ref, kb_ref, vt_ref, ko_ref, vo_ref = refs[pos:pos + 5]
    tm = x_ref.shape[0]
    first = 3 if tkv else 0
    h = _rms_rows(x_ref[...], g_ref[...]).astype(BF16)
    if has_extra:
        ox_ref[...] = _dot(h, wx_ref[...])
    for s, mode in enumerate(modes):
        cs = slice(s * SEC, (s + 1) * SEC)
        acc = _group_norm_cols(_dot(h, w_ref[:, cs]), gains_ref[:, cs], mode)
        if s >= first:
            o_ref[:, (s - first) * SEC:(s - first + 1) * SEC] = acc
            continue
        for hh in range(A_HEADS):
            a_h = acc[:, hh * LANES:(hh + 1) * LANES]
            if s == 0:
                qt_ref[0, hh] = a_h.T.astype(BF16)
            elif s == 1:
                ko_ref[0, 0, hh * LANES:(hh + 1) * LANES, :] = a_h.T
                for tt in range(tm // tkv):
                    kb_ref[0, hh, tt] = a_h[tt * tkv:(tt + 1) * tkv, :].astype(BF16)
            else:
                vo_ref[0, 0, pl.ds(hh, tm, stride=A_HEADS), :] = a_h
                for tt in range(tm // tkv):
                    vt_ref[0, hh, tt] = a_h[tt * tkv:(tt + 1) * tkv, :].T.astype(BF16)


def norm_proj(x, g, w, gains, modes, w_extra=None, tm=512, prefill=None):
    t, d = x.shape
    n = w.shape[1] // SEC
    tm = min(tm, t)
    assert t % tm == 0 and len(modes) == n

    def full(shape):
        return pl.BlockSpec(shape, lambda i: (0,) * len(shape))

    in_specs = [pl.BlockSpec((tm, d), lambda i: (i, 0)), full((1, d)), full((d, n * SEC)), full((1, n * SEC))]
    args = [x, g, w, gains]
    if w_extra is not None:
        in_specs.append(full((d, LANES)))
        args.append(w_extra)
    first, tkv, aliases = 0, 0, {}
    if prefill is not None:
        bsz, seq, tkv, layer, depth, k_state, v_state = prefill
        assert seq % tm == 0 and tm % tkv == 0 and bsz * seq == t
        first, nt = 3, seq // tm
    out_shape = [jax.ShapeDtypeStruct((t, (n - first) * SEC), F32)]
    out_specs = [pl.BlockSpec((tm, (n - first) * SEC), lambda i: (i, 0))]
    if w_extra is not None:
        out_shape.append(jax.ShapeDtypeStruct((t, LANES), F32))
        out_specs.append(pl.BlockSpec((tm, LANES), lambda i: (i, 0)))
    if prefill is not None:
        kpt = tm // tkv
        out_shape += [jax.ShapeDtypeStruct((bsz, A_HEADS, LANES, seq), BF16),
                      jax.ShapeDtypeStruct((bsz, A_HEADS, seq // tkv, tkv, LANES), BF16),
                      jax.ShapeDtypeStruct((bsz, A_HEADS, seq // tkv, LANES, tkv), BF16),
                      jax.ShapeDtypeStruct((depth, bsz, SEC, seq), F32),
                      jax.ShapeDtypeStruct((depth, bsz, seq * A_HEADS, A_DV), F32)]
        out_specs += [pl.BlockSpec((1, A_HEADS, LANES, tm), lambda i: (i // nt, 0, 0, i % nt)),
                      pl.BlockSpec((1, A_HEADS, kpt, tkv, LANES), lambda i: (i // nt, 0, i % nt, 0, 0)),
                      pl.BlockSpec((1, A_HEADS, kpt, LANES, tkv), lambda i: (i // nt, 0, i % nt, 0, 0)),
                      pl.BlockSpec((1, 1, SEC, tm), lambda i: (layer, i // nt, 0, i % nt)),
                      pl.BlockSpec((1, 1, tm * A_HEADS, A_DV), lambda i: (layer, i // nt, i % nt, 0))]
        if k_state is not None:
            n_out = len(out_shape)
            aliases = {len(args): n_out - 2, len(args) + 1: n_out - 1}
            in_specs += [pl.BlockSpec(memory_space=pl.ANY)] * 2
            args += [k_state, v_state]
    return pl.pallas_call(
        functools.partial(_norm_proj_kernel, modes=tuple(modes), has_extra=w_extra is not None,
                          n_alias=len(aliases), tkv=tkv),
        grid=(t // tm,), in_specs=in_specs, out_specs=out_specs, out_shape=out_shape,
        input_output_aliases=aliases,
        compiler_params=_cparams("parallel"), name="norm_proj")(*args)


def _attn_prefill_kernel(lam_ref, qt_ref, kb_ref, vt_ref, gs_ref, o_ref, s_scr, p_scr, a_scr,
                         m_scr, l_scr, acc_scr, *, tq, out_scale):
    qi = pl.program_id(2)
    kb = kb_ref.at[0, 0]
    vt = vt_ref.at[0, 0]
    qt = qt_ref[0, 0]
    drow = lax.broadcasted_iota(jnp.int32, qt.shape, 0)
    zero = jnp.zeros_like(qt)
    qq = jnp.concatenate([jnp.where(drow < A_DQK, qt, zero),
                          jnp.where(drow >= A_DQK, qt, zero)], axis=1)

    m_scr[...] = jnp.full(m_scr.shape, NEG, F32)
    l_scr[...] = jnp.zeros(l_scr.shape, F32)
    acc_scr[...] = jnp.zeros(acc_scr.shape, F32)
    p_scr[1] = jnp.zeros(p_scr.shape[1:], BF16)
    a_scr[1] = jnp.ones(a_scr.shape[1:], F32)

    def scores(j):
        return _dot(kb[j], qq)

    def softmax_stage(slot):
        s = s_scr[slot]
        m_old = m_scr[...]
        m_new = jnp.maximum(m_old, jnp.max(s, axis=0, keepdims=True))
        alpha = jnp.exp2(m_old - m_new)
        p = jnp.exp2(s - m_new)
        l_scr[...] = alpha * l_scr[...] + jnp.sum(p, axis=0, keepdims=True)
        m_scr[...] = m_new
        p_scr[slot] = p.astype(BF16)
        a_scr[slot] = alpha

    def accumulate_stage(slot, jv):
        acc_scr[...] = a_scr[slot] * acc_scr[...] + _dot(vt[jv], p_scr[slot])

    def kv_of_visit(k):
        return jnp.where(k <= 0, qi, k - 1)

    s = scores(qi)
    kpos = lax.broadcasted_iota(jnp.int32, s.shape, 0)
    qpos = lax.broadcasted_iota(jnp.int32, s.shape, 1) % tq
    s_scr[0] = jnp.where(kpos <= qpos, s, NEG)

    def body(j, c):
        slot = j % 2
        accumulate_stage(1 - slot, kv_of_visit(j - 1))
        s_next = scores(j)
        softmax_stage(slot)
        s_scr[1 - slot] = s_next
        return c

    lax.fori_loop(0, qi, body, 0)
    last = qi % 2
    accumulate_stage(1 - last, kv_of_visit(qi - 1))
    softmax_stage(last)
    accumulate_stage(last, kv_of_visit(qi))

    inv = 1.0 / l_scr[...]
    acc = acc_scr[...]
    o = acc[:, :tq] * inv[:, :tq] - lam_ref[0] * (acc[:, tq:] * inv[:, tq:])
    o = o * lax.rsqrt(jnp.mean(o * o, axis=0, keepdims=True) + EPS) * gs_ref[...] * out_scale
    o_ref[0] = o.T


def attn_prefill(qt, kb, vt, lam, g_subln, out_scale):
    b, _, nq, tq, _ = kb.shape
    t = nq * tq
    return pl.pallas_call(
        functools.partial(_attn_prefill_kernel, tq=tq, out_scale=out_scale),
        grid=(b, A_HEADS, nq),
        in_specs=[pl.BlockSpec(memory_space=pltpu.SMEM),
                  pl.BlockSpec((1, 1, LANES, tq), lambda bi, h, qi: (bi, h, 0, qi)),
                  pl.BlockSpec((1, 1, nq, tq, LANES), lambda bi, h, qi: (bi, h, 0, 0, 0)),
                  pl.BlockSpec((1, 1, nq, LANES, tq), lambda bi, h, qi: (bi, h, 0, 0, 0)),
                  pl.BlockSpec((A_DV, 1), lambda bi, h, qi: (0, 0))],
        out_specs=pl.BlockSpec((1, tq, LANES), lambda bi, h, qi: (bi, qi, h)),
        out_shape=jax.ShapeDtypeStruct((b, t, SEC), F32),
        scratch_shapes=[pltpu.VMEM((2, tq, 2 * tq), F32), pltpu.VMEM((2, tq, 2 * tq), BF16),
                        pltpu.VMEM((2, 1, 2 * tq), F32),
                        pltpu.VMEM((1, 2 * tq), F32), pltpu.VMEM((1, 2 * tq), F32),
                        pltpu.VMEM((A_DV, 2 * tq), F32)],
        compiler_params=_cparams("parallel", "parallel", "arbitrary"), name="attn_prefill",
    )(lam, qt, kb, vt, g_subln.reshape(A_DV, 1))


def _attn_decode_kernel(pt_ref, lam_ref, q_ref, kn_ref, vn_ref, gs_ref, *refs, pp, out_scale):
    k_refs, v_refs = refs[:pp], refs[pp:2 * pp]
    o_ref, qcol_scr, p_scr, a_scr, m_scr, l_scr, acc_scr = refs[2 * pp:]
    s_id = pl.program_id(1)
    n_steps = pl.num_programs(1) - 1
    n_rows = 2 * A_HEADS
    page = k_refs[0].shape[1]
    row1 = lax.broadcasted_iota(jnp.int32, (n_rows, 1), 0)

    @pl.when(s_id == 0)
    def _():
        eye = (lax.broadcasted_iota(jnp.int32, (LANES, LANES), 0)
               == lax.broadcasted_iota(jnp.int32, (LANES, LANES), 1))
        for blk in range(SEC // LANES):
            qb = q_ref[0][:, blk * LANES:(blk + 1) * LANES]
            q_col = jnp.sum(jnp.where(eye, qb, 0.0), axis=1, keepdims=True)
            qcol_scr[blk * LANES:(blk + 1) * LANES, :] = jnp.broadcast_to(q_col, (LANES, page))
        m_scr[...] = jnp.full(m_scr.shape, NEG, F32)
        l_scr[...] = jnp.zeros(l_scr.shape, F32)
        acc_scr[...] = jnp.zeros(acc_scr.shape, F32)
        p_scr[...] = jnp.zeros(p_scr.shape, BF16)
        a_scr[...] = jnp.ones(a_scr.shape, F32)

    pb = p_scr[...]
    pv = jnp.zeros((n_rows, A_DV), F32)
    for i in range(pp):
        p_i = pb[:, i * page:(i + 1) * page]
        for h in range(A_HEADS):
            v_h = v_refs[i][pl.ds(h, page, stride=A_HEADS), :]
            pv = pv + jnp.where(row1 // 2 == h, _dot(p_i, v_h.astype(BF16)), 0.0)
    acc_scr[...] = a_scr[...] * acc_scr[...] + pv

    qcol = qcol_scr[...]
    s_pages = []
    for i in range(pp):
        prod = k_refs[i][...] * qcol
        s_pages.append(jnp.concatenate(
            [jnp.sum(prod[r * A_DQK:(r + 1) * A_DQK, :], axis=0, keepdims=True) for r in range(n_rows)], axis=0))
    s = jnp.concatenate(s_pages, axis=1)
    s = jnp.where(s_id < n_steps, s, NEG)
    m_old = m_scr[...]
    m_new = jnp.maximum(m_old, jnp.max(s, axis=1, keepdims=True))
    alpha = jnp.exp2(m_old - m_new)
    p = jnp.exp2(s - m_new)
    l_scr[...] = alpha * l_scr[...] + jnp.sum(p, axis=1, keepdims=True)
    m_scr[...] = m_new
    p_scr[...] = p.astype(BF16)
    a_scr[...] = alpha

    @pl.when(s_id == n_steps)
    def _():
        row = lax.broadcasted_iota(jnp.int32, (n_rows, SEC), 0)
        col = lax.broadcasted_iota(jnp.int32, (n_rows, SEC), 1)
        qm = jnp.where(col // A_DQK == row, q_ref[0], 0.0)
        s_self = jnp.sum(qm * kn_ref[0], axis=1, keepdims=True)
        vn = vn_ref[0]
        vn_rows = jnp.concatenate([vn[:, (r // 2) * A_DV:(r // 2 + 1) * A_DV] for r in range(n_rows)], axis=0)
        m_old = m_scr[...]
        m_fin = jnp.maximum(m_old, s_self)
        alpha = jnp.exp2(m_old - m_fin)
        p_self = jnp.exp2(s_self - m_fin)
        l_fin = alpha * l_scr[...] + p_self
        acc = alpha * acc_scr[...] + p_self * vn_rows
        t = acc * (jnp.where(row1 % 2 == 0, 1.0, -lam_ref[0]) / l_fin)
        outs = []
        for h in range(A_HEADS):
            oh = t[2 * h:2 * h + 1, :] + t[2 * h + 1:2 * h + 2, :]
            outs.append(_rms_rows(oh, gs_ref[...]) * out_scale)
        o_ref[0] = jnp.concatenate(outs, axis=1)


def attn_decode(z, cache_kt, cache_v, page_table, layer, lam, g_subln, out_scale, pp=16):
    b = z.shape[0]
    n_pages = page_table.shape[1]
    page = cache_kt.shape[3]
    while n_pages % pp:
        pp //= 2

    n_steps = n_pages // pp

    def page_spec(i, shape, lag):
        def index(bi, s, pt):
            step = jnp.clip(s - lag, 0, n_steps - 1)
            return (layer, pt[bi, step * pp + i], 0, 0)
        return pl.BlockSpec((None, None) + shape, index)

    def row_spec(c):
        return pl.BlockSpec((1, 1, SEC), lambda bi, s, pt: (bi, 0, c))

    grid_spec = pltpu.PrefetchScalarGridSpec(
        num_scalar_prefetch=1, grid=(b, n_steps + 1),
        in_specs=[pl.BlockSpec(memory_space=pltpu.SMEM), row_spec(C_AQ), row_spec(C_AK), row_spec(C_AV),
                  pl.BlockSpec((1, A_DV), lambda bi, s, pt: (0, 0))]
                 + [page_spec(i, (SEC, page), 0) for i in range(pp)]
                 + [page_spec(i, (page * A_HEADS, A_DV), 1) for i in range(pp)],
        out_specs=pl.BlockSpec((1, 1, SEC), lambda bi, s, pt: (bi, 0, 0)),
        scratch_shapes=[pltpu.VMEM((SEC, page), F32),
                        pltpu.VMEM((2 * A_HEADS, pp * page), BF16), pltpu.VMEM((2 * A_HEADS, 1), F32),
                        pltpu.VMEM((2 * A_HEADS, 1), F32), pltpu.VMEM((2 * A_HEADS, 1), F32),
                        pltpu.VMEM((2 * A_HEADS, A_DV), F32)])
    return pl.pallas_call(
        functools.partial(_attn_decode_kernel, pp=pp, out_scale=out_scale),
        grid_spec=grid_spec, out_shape=jax.ShapeDtypeStruct((b, 1, SEC), F32),
        compiler_params=_cparams("parallel", "arbitrary"), name="attn_decode",
    )(page_table, lam, z, z, z, g_subln.reshape(1, A_DV), *([cache_kt] * pp), *([cache_v] * pp))


def _mlstm_prefill_kernel(zq_ref, zk_ref, v_ref, mo_ref, gif_ref, wc_ref, bc_ref, bif_ref, gmh_ref,
                          hm_ref, c_ref, n_ref, m_ref, conv_ref, xs, ct, n_scr, m_scr, *, chunk):
    c_id = pl.program_id(1)
    last = pl.num_programs(1) - 1
    L = chunk
    hw = M_HEADS * M_DH
    pad = SUBLANES

    @pl.when(c_id == 0)
    def _():
        xs[0:pad, :] = jnp.zeros((pad, 2 * hw), F32)
        ct[...] = jnp.zeros(ct.shape, F32)
        n_scr[...] = jnp.zeros(n_scr.shape, F32)
        m_scr[...] = jnp.zeros(m_scr.shape, F32)

    xs[pad:pad + L, 0:hw] = zq_ref[0]
    xs[pad:pad + L, hw:2 * hw] = zk_ref[0]
    y = bc_ref[...]
    for j in range(CONV_W):
        off = pad - (CONV_W - 1) + j
        y = y + xs[off:off + L, :] * wc_ref[j:j + 1, :]
    cq = y * jax.nn.sigmoid(y)
    xs[pad - (CONV_W - 1):pad, :] = xs[pad + L - (CONV_W - 1):pad + L, :]

    @pl.when(c_id == last)
    def _():
        conv_ref[0] = xs[pad - (CONV_W - 1):pad, :]

    g = gif_ref[0] + bif_ref[...]
    lf = _log_sigmoid(g)
    rows = lax.broadcasted_iota(jnp.int32, (L, L), 0)
    cols = lax.broadcasted_iota(jnp.int32, (L, L), 1)
    causal = rows >= cols
    tril = jnp.where(causal, 1.0, 0.0).astype(BF16)
    hi = lf.astype(BF16)
    r1 = lf - hi.astype(F32)
    mid = r1.astype(BF16)
    lo = (r1 - mid.astype(F32)).astype(BF16)
    bcum = _dot(tril, hi) + _dot(tril, mid) + _dot(tril, lo)
    bcum_t = bcum.T
    g_t = g.T

    for h in range(M_HEADS):
        hs = slice(h * M_DH, (h + 1) * M_DH)
        b_col = bcum[:, M_HEADS + h:M_HEADS + h + 1]
        b_row = bcum_t[M_HEADS + h:M_HEADS + h + 1, :]
        li_col = g[:, h:h + 1]
        li_row = g_t[h:h + 1, :]
        m_prev = m_scr[h][:, 0:1]
        log_d = jnp.where(causal, b_col - b_row + li_row, NEG)
        log_inter = b_col + m_prev
        m_t = jnp.maximum(log_inter, jnp.max(log_d, axis=1, keepdims=True))
        d = jnp.exp(log_d - m_t)
        w_inter = jnp.exp(log_inter - m_t)
        qh = cq[:, hs]
        kh = cq[:, hw + h * M_DH:hw + (h + 1) * M_DH] * (M_DH ** -0.5)
        vh = v_ref[0][:, hs]
        qb = qh.astype(BF16)
        s = _dot_nt(qb, kh.astype(BF16)) * d
        ct_h = ct[h]
        num = w_inter * _dot(qb, ct_h.astype(BF16)) + _dot(s.astype(BF16), vh.astype(BF16))
        n_prev = n_scr[h:h + 1, :]
        den = w_inter * jnp.sum(qh * n_prev, axis=1, keepdims=True) + jnp.sum(s, axis=1, keepdims=True)
        hh = num / jnp.maximum(jnp.abs(den), jnp.exp(-m_t))
        hm_ref[0, :, hs] = jax.nn.sigmoid(mo_ref[0][:, hs]) * _rms_rows(hh, gmh_ref[...])
        m_new = m_t[L - 1:L, :]
        b_last = b_col[L - 1:L, :]
        w_c = jnp.exp(b_last + m_prev - m_new)
        w_s = jnp.exp(b_last - b_col + li_col - m_new)
        ct[h] = w_c * ct_h + _dot(kh.T.astype(BF16), (w_s * vh).astype(BF16))
        n_scr[h:h + 1, :] = w_c * n_prev + jnp.sum(w_s * kh, axis=0, keepdims=True)
        m_scr[h] = jnp.broadcast_to(m_new, (1, LANES))

    @pl.when(c_id == last)
    def _():
        for h in range(M_HEADS):
            c_ref[0, h] = ct[h].T
            m_ref[0, h:h + 1, :] = m_scr[h]
        n_ref[0] = n_scr[0:M_HEADS, :]


def mlstm_prefill(z, gif, w_conv, b_conv, bif_row, g_mh, chunk=128):
    b, t, _ = z.shape
    chunk = min(chunk, t)
    assert t % chunk == 0 and t >= CONV_W - 1
    nc = t // chunk
    hw = M_HEADS * M_DH

    def zspec(c):
        return pl.BlockSpec((1, chunk, SEC), lambda bi, ci: (bi, ci, c - C_MQ))

    def full(shape):
        return pl.BlockSpec(shape, lambda bi, ci: (0,) * len(shape))

    return pl.pallas_call(
        functools.partial(_mlstm_prefill_kernel, chunk=chunk),
        grid=(b, nc),
        in_specs=[zspec(C_MQ), zspec(C_MK), zspec(C_MV), zspec(C_MO),
                  pl.BlockSpec((1, chunk, LANES), lambda bi, ci: (bi, ci, 0)),
                  full((CONV_W, 2 * hw)), full((1, 2 * hw)), full((1, LANES)), full((1, M_DH))],
        out_specs=[pl.BlockSpec((1, chunk, hw), lambda bi, ci: (bi, ci, 0)),
                   pl.BlockSpec((1, M_HEADS, M_DH, M_DH), lambda bi, ci: (bi, 0, 0, 0)),
                   pl.BlockSpec((1, M_HEADS, M_DH), lambda bi, ci: (bi, 0, 0)),
                   pl.BlockSpec((1, M_HEADS, LANES), lambda bi, ci: (bi, 0, 0)),
                   pl.BlockSpec((1, CONV_W - 1, 2 * hw), lambda bi, ci: (bi, 0, 0))],
        out_shape=[jax.ShapeDtypeStruct((b, t, hw), F32),
                   jax.ShapeDtypeStruct((b, M_HEADS, M_DH, M_DH), F32),
                   jax.ShapeDtypeStruct((b, M_HEADS, M_DH), F32),
                   jax.ShapeDtypeStruct((b, M_HEADS, LANES), F32),
                   jax.ShapeDtypeStruct((b, CONV_W - 1, 2 * hw), F32)],
        scratch_shapes=[pltpu.VMEM((chunk + SUBLANES, 2 * hw), F32),
                        pltpu.VMEM((M_HEADS, M_DH, M_DH), F32),
                        pltpu.VMEM((SUBLANES, M_DH), F32),
                        pltpu.VMEM((M_HEADS, 1, LANES), F32)],
        compiler_params=_cparams("parallel", "arbitrary"), name="mlstm_prefill",
    )(z, z, z, z, gif, w_conv, b_conv, bif_row, g_mh)


def _mlstm_decode_kernel(zq_ref, zk_ref, v_ref, mo_ref, gif_ref, buf_ref, c0_ref, n0_ref, m0_ref,
                         wc_ref, bc_ref, bif_ref, gmh_ref, hm_ref, c_ref, n_ref, m_ref, conv_ref):
    hw = M_HEADS * M_DH
    u = jnp.concatenate([zq_ref[0], zk_ref[0]], axis=1)
    buf = buf_ref[0]
    y = bc_ref[...]
    for j in range(CONV_W - 1):
        y = y + buf[j:j + 1, :] * wc_ref[j:j + 1, :]
    y = y + u * wc_ref[CONV_W - 1:CONV_W, :]
    cq = y * jax.nn.sigmoid(y)
    conv_ref[0] = jnp.concatenate([buf[1:CONV_W - 1, :], u], axis=0)

    g = gif_ref[0] + bif_ref[...]
    lf_all = _log_sigmoid(g)
    eye = (lax.broadcasted_iota(jnp.int32, (M_DH, M_DH), 0)
           == lax.broadcasted_iota(jnp.int32, (M_DH, M_DH), 1))
    v_all = v_ref[0]
    mo = mo_ref[0]
    m0 = m0_ref[0]
    outs, m_out = [], []
    for h in range(M_HEADS):
        hs = slice(h * M_DH, (h + 1) * M_DH)
        li = g[:, h:h + 1]
        lf = lf_all[:, M_HEADS + h:M_HEADS + h + 1]
        m_prev = m0[:, h:h + 1]
        log_inter = lf + m_prev
        m_t = jnp.maximum(log_inter, li)
        d = jnp.exp(li - m_t)
        w_inter = jnp.exp(log_inter - m_t)
        qh = cq[:, hs]
        kh = cq[:, hw + h * M_DH:hw + (h + 1) * M_DH] * (M_DH ** -0.5)
        vh = v_all[:, hs]
        s = jnp.sum(qh * kh, axis=1, keepdims=True) * d
        c_prev = c0_ref[0, h]
        cq_col = jnp.sum(c_prev * qh, axis=1, keepdims=True)
        v_col = jnp.sum(jnp.where(eye, vh, 0.0), axis=1, keepdims=True)
        num_col = w_inter * cq_col + s * v_col
        n_prev = n0_ref[0, h:h + 1, :]
        den = w_inter * jnp.sum(n_prev * qh, axis=1, keepdims=True) + s
        h_col = num_col / jnp.maximum(jnp.abs(den), jnp.exp(-m_t))
        h_row = jnp.sum(jnp.where(eye, h_col, 0.0), axis=0, keepdims=True)
        outs.append(jax.nn.sigmoid(mo[:, hs]) * _rms_rows(h_row, gmh_ref[...]))
        c_ref[0, h] = w_inter * c_prev + d * (v_col * kh)
        n_ref[0, h:h + 1, :] = w_inter * n_prev + d * kh
        m_out.append(m_t)
    hm_ref[0] = jnp.concatenate(outs, axis=1)
    lane = lax.broadcasted_iota(jnp.int32, (1, LANES), 1)
    m_row = jnp.zeros((1, LANES), F32)
    for h in range(M_HEADS):
        m_row = jnp.where(lane == h, m_out[h], m_row)
    m_ref[0] = m_row


def mlstm_decode(z, gif, buf, c0, n0, m0, w_conv, b_conv, bif_row, g_mh):
    b = z.shape[0]
    hw = M_HEADS * M_DH

    def zspec(c):
        return pl.BlockSpec((1, 1, SEC), lambda bi: (bi, 0, c))

    def per_b(shape):
        return pl.BlockSpec((1,) + shape, lambda bi: (bi,) + (0,) * len(shape))

    def full(shape):
        return pl.BlockSpec(shape, lambda bi: (0,) * len(shape))

    return pl.pallas_call(
        _mlstm_decode_kernel, grid=(b,),
        in_specs=[zspec(C_MQ), zspec(C_MK), zspec(C_MV), zspec(C_MO), per_b((1, LANES)),
                  per_b((CONV_W - 1, 2 * hw)), per_b((M_HEADS, M_DH, M_DH)), per_b((M_HEADS, M_DH)),
                  per_b((1, LANES)),
                  full((CONV_W, 2 * hw)), full((1, 2 * hw)), full((1, LANES)), full((1, M_DH))],
        out_specs=[per_b((1, hw)), per_b((M_HEADS, M_DH, M_DH)), per_b((M_HEADS, M_DH)),
                   per_b((1, LANES)), per_b((CONV_W - 1, 2 * hw))],
        out_shape=[jax.ShapeDtypeStruct((b, 1, hw), F32),
                   jax.ShapeDtypeStruct((b, M_HEADS, M_DH, M_DH), F32),
                   jax.ShapeDtypeStruct((b, M_HEADS, M_DH), F32),
                   jax.ShapeDtypeStruct((b, 1, LANES), F32),
                   jax.ShapeDtypeStruct((b, CONV_W - 1, 2 * hw), F32)],
        compiler_params=_cparams("parallel"), name="mlstm_decode",
    )(z, z, z, z, gif, buf, c0, n0, m0, w_conv, b_conv, bif_row, g_mh)


def _cross_attn_kernel(q_ref, k_ref, v_ref, o_ref):
    q = q_ref[0]
    tq = q.shape[0]
    if tq < SUBLANES:
        q = jnp.broadcast_to(q[0:1, :], (SUBLANES, q.shape[1]))
    k = k_ref[0].astype(BF16)
    v = v_ref[0].astype(BF16)
    for h in range(X_HEADS):
        hs = slice(h * X_DH, (h + 1) * X_DH)
        s = _dot_nt(q[:, hs].astype(BF16), k[:, hs]) * (X_DH ** -0.5)
        p = jnp.exp(s - jnp.max(s, axis=-1, keepdims=True))
        p = p / jnp.sum(p, axis=-1, keepdims=True)
        o = _dot(p.astype(BF16), v[:, hs])
        o_ref[0, :, hs] = o[0:tq, :]


def cross_attn(z, mk, mv, tq=512):
    b, t, zc = z.shape
    tq = min(tq, t)
    assert t % tq == 0
    n_mem = mk.shape[1]
    xq = zc // SEC - 1
    return pl.pallas_call(
        _cross_attn_kernel, grid=(b, t // tq),
        in_specs=[pl.BlockSpec((1, tq, SEC), lambda bi, qi: (bi, qi, xq)),
                  pl.BlockSpec((1, n_mem, SEC), lambda bi, qi: (bi, 0, 0)),
                  pl.BlockSpec((1, n_mem, SEC), lambda bi, qi: (bi, 0, 0))],
        out_specs=pl.BlockSpec((1, tq, SEC), lambda bi, qi: (bi, qi, 0)),
        out_shape=jax.ShapeDtypeStruct((b, t, SEC), F32),
        compiler_params=_cparams("parallel", "parallel"), name="cross_attn",
    )(z, mk, mv)


def _merge_kernel(x_ref, g_ref, oa_ref, hm_ref, ox_ref, wg_ref, wb_ref, wo_ref, o_ref):
    x = x_ref[...]
    d = x.shape[1]
    h = _rms_rows(x, g_ref[...]).astype(BF16)
    merged = None
    for i, br in enumerate((oa_ref, hm_ref, ox_ref)):
        gate = jax.nn.sigmoid(_dot(h, wg_ref[:, i * d:(i + 1) * d]))
        term = gate * _dot(br[...].astype(BF16), wb_ref[i])
        merged = term if merged is None else merged + term
    o_ref[...] = x + _dot(merged.astype(BF16), wo_ref[...])


def merge(x, g, oa, hm, ox, w_gate, w_branch, w_o, tm=256):
    t, d = x.shape
    tm = min(tm, t)
    assert t % tm == 0

    def rows(width):
        return pl.BlockSpec((tm, width), lambda i: (i, 0))

    def full(shape):
        return pl.BlockSpec(shape, lambda i: (0,) * len(shape))

    return pl.pallas_call(
        _merge_kernel, grid=(t // tm,),
        in_specs=[rows(d), full((1, d)), rows(SEC), rows(SEC), rows(SEC),
                  full(w_gate.shape), full(w_branch.shape), full(w_o.shape)],
        out_specs=rows(d), out_shape=jax.ShapeDtypeStruct((t, d), F32),
        compiler_params=_cparams("parallel"), name="merge",
    )(x, g, oa, hm, ox, w_gate, w_branch, w_o)


def _route_top2(h, wr_hi, wr_lo, b_router):
    h_hi = h.astype(BF16)
    h_lo = (h - h_hi.astype(F32)).astype(BF16)
    logits = _dot(h_hi, wr_hi) + (_dot(h_hi, wr_lo) + _dot(h_lo, wr_hi)) + b_router
    lane = lax.broadcasted_iota(jnp.int32, logits.shape, 1)
    logits = jnp.where(lane < N_EXPERTS, logits, -jnp.inf)
    v1 = jnp.max(logits, axis=-1, keepdims=True)
    i1 = jnp.min(jnp.where(logits == v1, lane, LANES), axis=-1, keepdims=True)
    rest = jnp.where(lane == i1, -jnp.inf, logits)
    v2 = jnp.max(rest, axis=-1, keepdims=True)
    i2 = jnp.min(jnp.where(rest == v2, lane, LANES), axis=-1, keepdims=True)
    e2 = jnp.exp(v2 - v1)
    den = 1.0 + e2
    gates = jnp.where(lane == i1, 1.0 / den, 0.0) + jnp.where(lane == i2, e2 / den, 0.0)
    return gates, jnp.logical_or(lane == i1, lane == i2)


def _swiglu(h, wg, wu, wd):
    gg = _dot(h, wg)
    uu = _dot(h, wu)
    return _dot(((gg * jax.nn.sigmoid(gg)) * uu).astype(BF16), wd)


def _ffn_kernel(*refs, routed, blk):
    if routed and blk:
        (x_ref, g_ref, wg_ref, wu_ref, wd_ref, wrh_ref, wrl_ref, br_ref, o_ref,
         h_scr, gate_scr, key_scr, keyt_scr, cnt_scr) = refs
    elif routed:
        x_ref, g_ref, wg_ref, wu_ref, wd_ref, wrh_ref, wrl_ref, br_ref, o_ref, h_scr, gate_scr = refs
    else:
        x_ref, g_ref, wg_ref, wu_ref, wd_ref, o_ref, h_scr = refs
    c = pl.program_id(1)
    tm = x_ref.shape[0]

    @pl.when(c == 0)
    def _():
        x = x_ref[...]
        h = _rms_rows(x, g_ref[...])
        h_scr[...] = h.astype(BF16)
        o_ref[...] = x
        if routed:
            gates, sel = _route_top2(h, wrh_ref[...], wrl_ref[...], br_ref[...])
            gate_scr[...] = gates
        if routed and blk:
            earlier = (lax.broadcasted_iota(jnp.int32, (tm, tm), 1)
                       < lax.broadcasted_iota(jnp.int32, (tm, tm), 0))
            sel01 = jnp.where(sel, 1.0, 0.0)
            rank = _dot(jnp.where(earlier, 1.0, 0.0).astype(BF16), sel01.astype(BF16))
            key = jnp.where(sel, rank, -1.0)
            key_scr[...] = key
            keyt_scr[...] = key.T
            cnt_scr[...] = jnp.sum(sel01, axis=0, keepdims=True)

    if not routed:
        o_ref[...] += _swiglu(h_scr[...], wg_ref[0], wu_ref[0], wd_ref[0])
        return
    lane = lax.broadcasted_iota(jnp.int32, gate_scr.shape, 1)
    gate_col = jnp.sum(jnp.where(lane == c, gate_scr[...], 0.0), axis=-1, keepdims=True)
    if not blk:
        o_ref[...] += gate_col * _swiglu(h_scr[...], wg_ref[0], wu_ref[0], wd_ref[0])
        return

    key_col = jnp.sum(jnp.where(lane == c, key_scr[...], 0.0), axis=-1, keepdims=True)
    key_row = keyt_scr[pl.ds(c, 1), :]
    lane1 = lax.broadcasted_iota(jnp.int32, cnt_scr.shape, 1)
    count = jnp.sum(jnp.where(lane1 == c, cnt_scr[...], 0.0)).astype(jnp.int32)
    slot_rows = lax.broadcasted_iota(jnp.int32, (blk, tm), 0).astype(F32)
    slot_cols = lax.broadcasted_iota(jnp.int32, (tm, blk), 1).astype(F32)
    half = tm // 2

    def block(b, carry):
        base = (b * blk).astype(F32)
        pick = jnp.where(key_row - base == slot_rows, 1.0, 0.0).astype(BF16)
        h_sel = _dot(pick, h_scr[...]).astype(BF16)
        y = _swiglu(h_sel, wg_ref[0], wu_ref[0], wd_ref[0]).astype(BF16)
        place = jnp.where(key_col - base == slot_cols, 1.0, 0.0).astype(BF16)
        for r in range(2):
            rows = slice(r * half, (r + 1) * half)
            o_ref[rows, :] += gate_col[rows, :] * _dot(place[rows, :], y)
        return carry

    lax.fori_loop(0, (count + blk - 1) // blk, block, 0)


def ffn(x, g, w_gate, w_up, w_down, router=None, tm=512, blk=256):
    t, d = x.shape
    if router is not None:
        tm = 2 * tm
    tm = min(tm, t)
    assert t % tm == 0
    blk = blk if (router is not None and tm >= 2 * blk) else 0
    wg_arr, wg_map = w_gate
    wu_arr, wu_map = w_up
    n_c, f, _ = w_down.shape
    in_specs = [pl.BlockSpec((tm, d), lambda i, c: (i, 0)),
                pl.BlockSpec((1, d), lambda i, c: (0, 0)),
                pl.BlockSpec((1, d, f), lambda i, c: wg_map(c)),
                pl.BlockSpec((1, d, f), lambda i, c: wu_map(c)),
                pl.BlockSpec((1, f, d), lambda i, c: (c, 0, 0))]
    args = [x, g, wg_arr, wu_arr, w_down]
    scratch = [pltpu.VMEM((tm, d), BF16)]
    if router is not None:
        in_specs += [pl.BlockSpec((d, LANES), lambda i, c: (0, 0)),
                     pl.BlockSpec((d, LANES), lambda i, c: (0, 0)),
                     pl.BlockSpec((1, LANES), lambda i, c: (0, 0))]
        args += list(router)
        scratch.append(pltpu.VMEM((tm, LANES), F32))
        if blk:
            scratch += [pltpu.VMEM((tm, LANES), F32), pltpu.VMEM((LANES, tm), F32), pltpu.VMEM((1, LANES), F32)]
    return pl.pallas_call(
        functools.partial(_ffn_kernel, routed=router is not None, blk=blk), grid=(t // tm, n_c),
        in_specs=in_specs, out_specs=pl.BlockSpec((tm, d), lambda i, c: (i, 0)),
        out_shape=jax.ShapeDtypeStruct((t, d), F32), scratch_shapes=scratch,
        compiler_params=_cparams("parallel", "arbitrary"), name="ffn")(*args)


def _pad_lanes(a):
    return jnp.pad(a, ((0, 0),) * (a.ndim - 1) + ((0, LANES - a.shape[-1]),))


def kernel(x_prompt, x_sample, mem_prompt, cache_k, cache_v, page_table, cache_mem_k, cache_mem_v, state_C, state_n, state_m, state_conv, g_attn_norm, w_in, b_if, g_q, g_k, lam_q1, lam_k1, lam_q2, lam_k2, g_subln, w_conv, b_conv, g_mh, g_mem_norm, w_mem_kv, g_mq, g_mk, w_branch, w_o, g_ffn_norm, w_dense_gu, w_dense_down, w_router, b_router, w_moe_gu, w_moe_down):
    depth = w_in.shape[0]
    bp, tp, d = x_prompt.shape
    bs, ts, _ = x_sample.shape
    assert ts == 1, "the sample group decodes one token per sequence"
    n_mem = mem_prompt.shape[1]
    hw = M_HEADS * M_DH
    n_pool, page = cache_k.shape[1], cache_k.shape[2]
    cache_kt = jnp.transpose(cache_k, (0, 1, 3, 4, 5, 2)).reshape(depth, n_pool, SEC, page)
    cache_v2 = cache_v.reshape(depth, n_pool, page * A_HEADS, A_DV)

    o_aq, o_ak, o_av, o_mqk = 0, 512, 1024, 1536
    o_mv, o_mo, o_mif, o_xq, o_g = 2560, 3072, 3584, 3592, 4104

    yp = x_prompt.reshape(bp * tp, d)
    ys = x_sample.reshape(bs * ts, d)
    mem = mem_prompt.reshape(bp * n_mem, d)
    outs = {k: [] for k in ("mkp", "mvp", "cp", "np", "mp", "convp", "ks", "vs", "cs", "ns", "ms", "convs")}
    k_state = v_state = None
    ones = jnp.ones((SEC,), F32)
    for l in range(depth):
        lam_init = 0.8 - 0.6 * math.exp(-0.3 * l)
        lam = (jnp.exp(jnp.sum(lam_q1[l] * lam_k1[l])) - jnp.exp(jnp.sum(lam_q2[l] * lam_k2[l]))
               + lam_init).reshape(1).astype(F32)
        wl = w_in[l]
        w_main = jnp.concatenate([wl[:, o_aq:o_mif], wl[:, o_xq:o_g]], axis=1).astype(BF16)
        w_if = _pad_lanes(wl[:, o_mif:o_xq]).astype(BF16)
        w_gate = wl[:, o_g:].astype(BF16)
        q_scale = (A_DQK ** -0.5) * math.log2(math.e)
        gains = jnp.concatenate([jnp.tile(g_q[l], SEC // A_DQK) * q_scale, jnp.tile(g_k[l], SEC // A_DQK),
                                 ones, ones, ones, ones, ones, jnp.tile(g_mq[l], SEC // X_DH)])[None, :]
        modes = (A_DQK, A_DQK, 0, 0, 0, 0, 0, X_DH)
        g_attn = g_attn_norm[l][None, :]
        bif_row = _pad_lanes(b_if[l][None, :])
        wc, bc, gmh = w_conv[l], b_conv[l][None, :], g_mh[l][None, :]
        wb, wo = w_branch[l].astype(BF16), w_o[l].astype(BF16)
        out_scale = 1.0 - lam_init

        kv_gains = jnp.concatenate([jnp.tile(g_mk[l], SEC // X_DH), ones])[None, :]
        mkv, = norm_proj(mem, g_mem_norm[l][None, :], w_mem_kv[l].astype(BF16), kv_gains, (X_DH, 0))
        mk_p = mkv[:, :SEC].reshape(bp, n_mem, SEC)
        mv_p = mkv[:, SEC:].reshape(bp, n_mem, SEC)

        z, gif, qt, kb, vt, k_state, v_state = norm_proj(
            yp, g_attn, w_main, gains, modes, w_extra=w_if,
            prefill=(bp, tp, min(256, tp), l, depth, k_state, v_state))
        z3 = z.reshape(bp, tp, z.shape[1])
        oa = attn_prefill(qt, kb, vt, lam, g_subln[l], out_scale)
        hm, c_p, n_p, m_p, conv_p = mlstm_prefill(z3, gif.reshape(bp, tp, LANES), wc, bc, bif_row, gmh)
        ox = cross_attn(z3, mk_p, mv_p)
        yp = merge(yp, g_attn, oa.reshape(bp * tp, SEC), hm.reshape(bp * tp, hw), ox.reshape(bp * tp, SEC),
                   w_gate, wb, wo)

        zs, gifs = norm_proj(ys, g_attn, w_main, gains, modes, w_extra=w_if)
        zs3 = zs.reshape(bs, 1, Z_COLS)
        oa_s = attn_decode(zs3, cache_kt, cache_v2, page_table, l, lam, g_subln[l], out_scale)
        hm_s, c_s, n_s, m_s, conv_s = mlstm_decode(
            zs3, gifs.reshape(bs, 1, LANES), state_conv[l], state_C[l], state_n[l],
            _pad_lanes(state_m[l])[:, None, :], wc, bc, bif_row, gmh)
        ox_s = cross_attn(zs3, cache_mem_k[l].reshape(bs, n_mem, SEC), cache_mem_v[l].reshape(bs, n_mem, SEC))
        ys = merge(ys, g_attn, oa_s.reshape(bs, SEC), hm_s.reshape(bs, hw), ox_s.reshape(bs, SEC),
                   w_gate, wb, wo)
        ka_s = zs3[:, :, C_AK * SEC:(C_AK + 1) * SEC].reshape(bs, 1, A_HEADS, 2, A_DQK)
        va_s = zs3[:, :, C_AV * SEC:(C_AV + 1) * SEC].reshape(bs, 1, A_HEADS, A_DV)

        i = l // 2
        g_ffn = g_ffn_norm[l][None, :]
        if l % 2 == 0:
            wgu = w_dense_gu[i].astype(BF16)[None]
            d_ff = w_dense_down.shape[1]
            n_c = 2 if d_ff % (2 * LANES) == 0 else 1
            f = d_ff // n_c
            w_g = (wgu, lambda c: (0, 0, c))
            w_u = (wgu, lambda c, n_c=n_c: (0, 0, n_c + c))
            w_d = w_dense_down[i].astype(BF16).reshape(n_c, f, d)
            yp = ffn(yp, g_ffn, w_g, w_u, w_d)
            ys = ffn(ys, g_ffn, w_g, w_u, w_d)
        else:
            wgu = w_moe_gu[i].astype(BF16)
            w_g = (wgu, lambda c: (c, 0, 0))
            w_u = (wgu, lambda c: (c, 0, 1))
            w_d = w_moe_down[i].astype(BF16)
            wr = _pad_lanes(w_router[i])
            wr_hi = wr.astype(BF16)
            wr_lo = (wr - wr_hi.astype(F32)).astype(BF16)
            router = (wr_hi, wr_lo, _pad_lanes(b_router[i][None, :]))
            yp = ffn(yp, g_ffn, w_g, w_u, w_d, router)
            ys = ffn(ys, g_ffn, w_g, w_u, w_d, router)

        for name, val in zip(outs, (mk_p.reshape(bp, n_mem, X_HEADS, X_DH),
                                    mv_p.reshape(bp, n_mem, X_HEADS, X_DH), c_p, n_p, m_p[:, :, 0], conv_p,
                                    ka_s, va_s, c_s, n_s, m_s[:, 0, :M_HEADS], conv_s)):
            outs[name].append(val)

    st = {name: jnp.stack(vals, axis=0) for name, vals in outs.items()}
    st["kp"] = jnp.transpose(k_state.reshape(depth, bp, A_HEADS, 2, A_DQK, tp), (0, 1, 5, 2, 3, 4))
    st["vp"] = v_state.reshape(depth, bp, tp, A_HEADS, A_DV)
    return (yp.reshape(bp, tp, d), ys.reshape(bs, ts, d), st["kp"], st["vp"], st["mkp"], st["mvp"],
            st["cp"], st["np"], st["mp"], st["convp"], st["ks"], st["vs"], st["cs"], st["ns"], st["ms"],
            st["convs"])
```

```python
import functools
import math

import jax
import jax.numpy as jnp
from jax import lax
from jax.experimental import pallas as pl
from jax.experimental.pallas import tpu as pltpu

F32 = jnp.float32
BF16 = jnp.bfloat16

EPS = 1e-6
NEG = -1e30

A_HEADS = 4
A_DQK = 64
A_DV = 128
M_HEADS = 4
M_DH = 128
CONV_W = 4
X_HEADS = 4
X_DH = 128
N_BRANCH = 3
N_EXPERTS = 8
LANES = 128
SUBLANES = 8

SEC = 512
C_AQ, C_AK, C_AV, C_MQ, C_MK, C_MV, C_MO, C_XQ = range(8)
Z_COLS = 8 * SEC

VMEM_LIMIT = 56 * 1024 * 1024


def _cparams(*sem):
    return pltpu.CompilerParams(dimension_semantics=sem, vmem_limit_bytes=VMEM_LIMIT)


def _dot(a, b):
    return jnp.dot(a, b, preferred_element_type=F32)


def _dot_nt(a, b):
    return lax.dot_general(a, b, (((1,), (1,)), ((), ())), preferred_element_type=F32)


def _rms_rows(x, g):
    return x * lax.rsqrt(jnp.mean(x * x, axis=-1, keepdims=True) + EPS) * g


def _log_sigmoid(x):
    return jnp.minimum(x, 0.0) - jnp.log1p(jnp.exp(-jnp.abs(x)))


def _group_norm_cols(zs, gains, group):
    if group == 0:
        return zs
    outs = []
    for i in range(zs.shape[1] // LANES):
        zb = zs[:, i * LANES:(i + 1) * LANES]
        sq = zb * zb
        if group == LANES:
            ms = jnp.mean(sq, axis=-1, keepdims=True)
        else:
            lane = lax.broadcasted_iota(jnp.int32, zb.shape, 1)
            low = lane < group
            s_lo = jnp.sum(jnp.where(low, sq, 0.0), axis=-1, keepdims=True)
            s_hi = jnp.sum(jnp.where(low, 0.0, sq), axis=-1, keepdims=True)
            ms = jnp.where(low, s_lo, s_hi) * (1.0 / group)
        outs.append(zb * lax.rsqrt(ms + EPS) * gains[:, i * LANES:(i + 1) * LANES])
    return jnp.concatenate(outs, axis=1)


def _norm_proj_kernel(*refs, modes, has_extra, n_alias, tkv):
    x_ref, g_ref, w_ref, gains_ref = refs[:4]
    pos = 4
    if has_extra:
        wx_ref = refs[pos]
        pos += 1
    pos += n_alias
    o_ref = refs[pos]
    pos += 1
    if has_extra:
        ox_ref = refs[pos]
        pos += 1
    if tkv:
        qt_ref, kb_ref, vt_ref, ko_ref, vo_ref = refs[pos:pos + 5]
    tm = x_ref.shape[0]
    first = 3 if tkv else 0
    h = _rms_rows(x_ref[...], g_ref[...]).astype(BF16)
    if has_extra:
        ox_ref[...] = _dot(h, wx_ref[...])
    for s, mode in enumerate(modes):
        cs = slice(s * SEC, (s + 1) * SEC)
        acc = _group_norm_cols(_dot(h, w_ref[:, cs]), gains_ref[:, cs], mode)
        if s >= first:
            o_ref[:, (s - first) * SEC:(s - first + 1) * SEC] = acc
            continue
        for hh in range(A_HEADS):
            a_h = acc[:, hh * LANES:(hh + 1) * LANES]
            if s == 0:
                qt_ref[0, hh] = a_h.T.astype(BF16)
            elif s == 1:
                ko_ref[0, 0, hh * LANES:(hh + 1) * LANES, :] = a_h.T
                for tt in range(tm // tkv):
                    kb_ref[0, hh, tt] = a_h[tt * tkv:(tt + 1) * tkv, :].astype(BF16)
            else:
                vo_ref[0, 0, pl.ds(hh, tm, stride=A_HEADS), :] = a_h
                for tt in range(tm // tkv):
                    vt_ref[0, hh, tt] = a_h[tt * tkv:(tt + 1) * tkv, :].T.astype(BF16)


def norm_proj(x, g, w, gains, modes, w_extra=None, tm=512, prefill=None):
    t, d = x.shape
    n = w.shape[1] // SEC
    tm = min(tm, t)
    assert t % tm == 0 and len(modes) == n

    def full(shape):
        return pl.BlockSpec(shape, lambda i: (0,) * len(shape))

    in_specs = [pl.BlockSpec((tm, d), lambda i: (i, 0)), full((1, d)), full((d, n * SEC)), full((1, n * SEC))]
    args = [x, g, w, gains]
    if w_extra is not None:
        in_specs.append(full((d, LANES)))
        args.append(w_extra)
    first, tkv, aliases = 0, 0, {}
    if prefill is not None:
        bsz, seq, tkv, layer, depth, k_state, v_state = prefill
        assert seq % tm == 0 and tm % tkv == 0 and bsz * seq == t
        first, nt = 3, seq // tm
    out_shape = [jax.ShapeDtypeStruct((t, (n - first) * SEC), F32)]
    out_specs = [pl.BlockSpec((tm, (n - first) * SEC), lambda i: (i, 0))]
    if w_extra is not None:
        out_shape.append(jax.ShapeDtypeStruct((t, LANES), F32))
        out_specs.append(pl.BlockSpec((tm, LANES), lambda i: (i, 0)))
    if prefill is not None:
        kpt = tm // tkv
        out_shape += [jax.ShapeDtypeStruct((bsz, A_HEADS, LANES, seq), BF16),
                      jax.ShapeDtypeStruct((bsz, A_HEADS, seq // tkv, tkv, LANES), BF16),
                      jax.ShapeDtypeStruct((bsz, A_HEADS, seq // tkv, LANES, tkv), BF16),
                      jax.ShapeDtypeStruct((depth, bsz, SEC, seq), F32),
                      jax.ShapeDtypeStruct((depth, bsz, seq * A_HEADS, A_DV), F32)]
        out_specs += [pl.BlockSpec((1, A_HEADS, LANES, tm), lambda i: (i // nt, 0, 0, i % nt)),
                      pl.BlockSpec((1, A_HEADS, kpt, tkv, LANES), lambda i: (i // nt, 0, i % nt, 0, 0)),
                      pl.BlockSpec((1, A_HEADS, kpt, LANES, tkv), lambda i: (i // nt, 0, i % nt, 0, 0)),
                      pl.BlockSpec((1, 1, SEC, tm), lambda i: (layer, i // nt, 0, i % nt)),
                      pl.BlockSpec((1, 1, tm * A_HEADS, A_DV), lambda i: (layer, i // nt, i % nt, 0))]
        if k_state is not None:
            n_out = len(out_shape)
            aliases = {len(args): n_out - 2, len(args) + 1: n_out - 1}
            in_specs += [pl.BlockSpec(memory_space=pl.ANY)] * 2
            args += [k_state, v_state]
    return pl.pallas_call(
        functools.partial(_norm_proj_kernel, modes=tuple(modes), has_extra=w_extra is not None,
                          n_alias=len(aliases), tkv=tkv),
        grid=(t // tm,), in_specs=in_specs, out_specs=out_specs, out_shape=out_shape,
        input_output_aliases=aliases,
        compiler_params=_cparams("parallel"), name="norm_proj")(*args)


def _attn_prefill_kernel(lam_ref, qt_ref, kb_ref, vt_ref, gs_ref, o_ref, s_scr, p_scr, a_scr,
                         m_scr, l_scr, acc_scr, *, tq, out_scale):
    qi = pl.program_id(2)
    kb = kb_ref.at[0, 0]
    vt = vt_ref.at[0, 0]
    qt = qt_ref[0, 0]
    drow = lax.broadcasted_iota(jnp.int32, qt.shape, 0)
    zero = jnp.zeros_like(qt)
    qq = jnp.concatenate([jnp.where(drow < A_DQK, qt, zero),
                          jnp.where(drow >= A_DQK, qt, zero)], axis=1)

    m_scr[...] = jnp.full(m_scr.shape, NEG, F32)
    l_scr[...] = jnp.zeros(l_scr.shape, F32)
    acc_scr[...] = jnp.zeros(acc_scr.shape, F32)
    p_scr[1] = jnp.zeros(p_scr.shape[1:], BF16)
    a_scr[1] = jnp.ones(a_scr.shape[1:], F32)

    def scores(j):
        return _dot(kb[j], qq)

    def softmax_stage(slot):
        s = s_scr[slot]
        m_old = m_scr[...]
        m_new = jnp.maximum(m_old, jnp.max(s, axis=0, keepdims=True))
        alpha = jnp.exp2(m_old - m_new)
        p = jnp.exp2(s - m_new)
        l_scr[...] = alpha * l_scr[...] + jnp.sum(p, axis=0, keepdims=True)
        m_scr[...] = m_new
        p_scr[slot] = p.astype(BF16)
        a_scr[slot] = alpha

    def accumulate_stage(slot, jv):
        acc_scr[...] = a_scr[slot] * acc_scr[...] + _dot(vt[jv], p_scr[slot])

    def kv_of_visit(k):
        return jnp.where(k <= 0, qi, k - 1)

    s = scores(qi)
    kpos = lax.broadcasted_iota(jnp.int32, s.shape, 0)
    qpos = lax.broadcasted_iota(jnp.int32, s.shape, 1) % tq
    s_scr[0] = jnp.where(kpos <= qpos, s, NEG)

    def body(j, c):
        slot = j % 2
        accumulate_stage(1 - slot, kv_of_visit(j - 1))
        s_next = scores(j)
        softmax_stage(slot)
        s_scr[1 - slot] = s_next
        return c

    lax.fori_loop(0, qi, body, 0)
    last = qi % 2
    accumulate_stage(1 - last, kv_of_visit(qi - 1))
    softmax_stage(last)
    accumulate_stage(last, kv_of_visit(qi))

    inv = 1.0 / l_scr[...]
    acc = acc_scr[...]
    o = acc[:, :tq] * inv[:, :tq] - lam_ref[0] * (acc[:, tq:] * inv[:, tq:])
    o = o * lax.rsqrt(jnp.mean(o * o, axis=0, keepdims=True) + EPS) * gs_ref[...] * out_scale
    o_ref[0] = o.T


def attn_prefill(qt, kb, vt, lam, g_subln, out_scale):
    b, _, nq, tq, _ = kb.shape
    t = nq * tq

    return pl.pallas_call(
        functools.partial(_attn_prefill_kernel, tq=tq, out_scale=out_scale),
        grid=(b, A_HEADS, nq),
        in_specs=[pl.BlockSpec(memory_space=pltpu.SMEM),
                  pl.BlockSpec((1, 1, LANES, tq), lambda bi, h, qi: (bi, h, 0, qi)),
                  pl.BlockSpec((1, 1, nq, tq, LANES), lambda bi, h, qi: (bi, h, 0, 0, 0)),
                  pl.BlockSpec((1, 1, nq, LANES, tq), lambda bi, h, qi: (bi, h, 0, 0, 0)),
                  pl.BlockSpec((A_DV, 1), lambda bi, h, qi: (0, 0))],
        out_specs=pl.BlockSpec((1, tq, LANES), lambda bi, h, qi: (bi, qi, h)),
        out_shape=jax.ShapeDtypeStruct((b, t, SEC), F32),
        scratch_shapes=[pltpu.VMEM((2, tq, 2 * tq), F32), pltpu.VMEM((2, tq, 2 * tq), BF16),
                        pltpu.VMEM((2, 1, 2 * tq), F32),
                        pltpu.VMEM((1, 2 * tq), F32), pltpu.VMEM((1, 2 * tq), F32),
                        pltpu.VMEM((A_DV, 2 * tq), F32)],
        compiler_params=_cparams("parallel", "parallel", "arbitrary"), name="attn_prefill",
    )(lam, qt, kb, vt, g_subln.reshape(A_DV, 1))


def _attn_decode_kernel(pt_ref, lam_ref, q_ref, kn_ref, vn_ref, gs_ref, *refs, pp, out_scale):
    k_refs, v_refs = refs[:pp], refs[pp:2 * pp]
    o_ref, qcol_scr, p_scr, a_scr, m_scr, l_scr, acc_scr = refs[2 * pp:]
    s_id = pl.program_id(1)
    n_steps = pl.num_programs(1) - 1
    n_rows = 2 * A_HEADS
    page = k_refs[0].shape[1]
    row1 = lax.broadcasted_iota(jnp.int32, (n_rows, 1), 0)

    @pl.when(s_id == 0)
    def _():
        eye = (lax.broadcasted_iota(jnp.int32, (LANES, LANES), 0)
               == lax.broadcasted_iota(jnp.int32, (LANES, LANES), 1))
        for blk in range(SEC // LANES):
            qb = q_ref[0][:, blk * LANES:(blk + 1) * LANES]
            q_col = jnp.sum(jnp.where(eye, qb, 0.0), axis=1, keepdims=True)
            qcol_scr[blk * LANES:(blk + 1) * LANES, :] = jnp.broadcast_to(q_col, (LANES, page))
        m_scr[...] = jnp.full(m_scr.shape, NEG, F32)
        l_scr[...] = jnp.zeros(l_scr.shape, F32)
        acc_scr[...] = jnp.zeros(acc_scr.shape, F32)
        p_scr[...] = jnp.zeros(p_scr.shape, BF16)
        a_scr[...] = jnp.ones(a_scr.shape, F32)

    pb = p_scr[...]
    pv = jnp.zeros((n_rows, A_DV), F32)
    for i in range(pp):
        p_i = pb[:, i * page:(i + 1) * page]
        for h in range(A_HEADS):
            v_h = v_refs[i][pl.ds(h, page, stride=A_HEADS), :]
            pv = pv + jnp.where(row1 // 2 == h, _dot(p_i, v_h.astype(BF16)), 0.0)
    acc_scr[...] = a_scr[...] * acc_scr[...] + pv

    qcol = qcol_scr[...]
    s_pages = []
    for i in range(pp):
        prod = k_refs[i][...] * qcol
        s_pages.append(jnp.concatenate(
            [jnp.sum(prod[r * A_DQK:(r + 1) * A_DQK, :], axis=0, keepdims=True) for r in range(n_rows)], axis=0))
    s = jnp.concatenate(s_pages, axis=1)
    s = jnp.where(s_id < n_steps, s, NEG)
    m_old = m_scr[...]
    m_new = jnp.maximum(m_old, jnp.max(s, axis=1, keepdims=True))
    alpha = jnp.exp2(m_old - m_new)
    p = jnp.exp2(s - m_new)
    l_scr[...] = alpha * l_scr[...] + jnp.sum(p, axis=1, keepdims=True)
    m_scr[...] = m_new
    p_scr[...] = p.astype(BF16)
    a_scr[...] = alpha

    @pl.when(s_id == n_steps)
    def _():
        row = lax.broadcasted_iota(jnp.int32, (n_rows, SEC), 0)
        col = lax.broadcasted_iota(jnp.int32, (n_rows, SEC), 1)
        qm = jnp.where(col // A_DQK == row, q_ref[0], 0.0)
        s_self = jnp.sum(qm * kn_ref[0], axis=1, keepdims=True)
        vn = vn_ref[0]
        vn_rows = jnp.concatenate([vn[:, (r // 2) * A_DV:(r // 2 + 1) * A_DV] for r in range(n_rows)], axis=0)
        m_old = m_scr[...]
        m_fin = jnp.maximum(m_old, s_self)
        alpha = jnp.exp2(m_old - m_fin)
        p_self = jnp.exp2(s_self - m_fin)
        l_fin = alpha * l_scr[...] + p_self
        acc = alpha * acc_scr[...] + p_self * vn_rows
        t = acc * (jnp.where(row1 % 2 == 0, 1.0, -lam_ref[0]) / l_fin)
        outs = []
        for h in range(A_HEADS):
            oh = t[2 * h:2 * h + 1, :] + t[2 * h + 1:2 * h + 2, :]
            outs.append(_rms_rows(oh, gs_ref[...]) * out_scale)
        o_ref[0] = jnp.concatenate(outs, axis=1)


def attn_decode(z, cache_kt, cache_v, page_table, layer, lam, g_subln, out_scale, pp=16):
    b = z.shape[0]
    n_pages = page_table.shape[1]
    page = cache_kt.shape[3]
    while n_pages % pp:
        pp //= 2

    n_steps = n_pages // pp

    def page_spec(i, shape, lag):
        def index(bi, s, pt):
            step = jnp.clip(s - lag, 0, n_steps - 1)
            return (layer, pt[bi, step * pp + i], 0, 0)
        return pl.BlockSpec((None, None) + shape, index)

    def row_spec(c):
        return pl.BlockSpec((1, 1, SEC), lambda bi, s, pt: (bi, 0, c))

    grid_spec = pltpu.PrefetchScalarGridSpec(
        num_scalar_prefetch=1, grid=(b, n_steps + 1),
        in_specs=[pl.BlockSpec(memory_space=pltpu.SMEM), row_spec(C_AQ), row_spec(C_AK), row_spec(C_AV),
                  pl.BlockSpec((1, A_DV), lambda bi, s, pt: (0, 0))]
                 + [page_spec(i, (SEC, page), 0) for i in range(pp)]
                 + [page_spec(i, (page * A_HEADS, A_DV), 1) for i in range(pp)],
        out_specs=pl.BlockSpec((1, 1, SEC), lambda bi, s, pt: (bi, 0, 0)),
        scratch_shapes=[pltpu.VMEM((SEC, page), F32),
                        pltpu.VMEM((2 * A_HEADS, pp * page), BF16), pltpu.VMEM((2 * A_HEADS, 1), F32),
                        pltpu.VMEM((2 * A_HEADS, 1), F32), pltpu.VMEM((2 * A_HEADS, 1), F32),
                        pltpu.VMEM((2 * A_HEADS, A_DV), F32)])
    return pl.pallas_call(
        functools.partial(_attn_decode_kernel, pp=pp, out_scale=out_scale),
        grid_spec=grid_spec, out_shape=jax.ShapeDtypeStruct((b, 1, SEC), F32),
        compiler_params=_cparams("parallel", "arbitrary"), name="attn_decode",
    )(page_table, lam, z, z, z, g_subln.reshape(1, A_DV), *([cache_kt] * pp), *([cache_v] * pp))


def _mlstm_prefill_kernel(zq_ref, zk_ref, v_ref, mo_ref, gif_ref, wc_ref, bc_ref, bif_ref, gmh_ref,
                          hm_ref, c_ref, n_ref, m_ref, conv_ref, xs, ct, n_scr, m_scr, *, chunk):
    c_id = pl.program_id(1)
    last = pl.num_programs(1) - 1
    L = chunk
    hw = M_HEADS * M_DH
    pad = SUBLANES

    @pl.when(c_id == 0)
    def _():
        xs[0:pad, :] = jnp.zeros((pad, 2 * hw), F32)
        ct[...] = jnp.zeros(ct.shape, F32)
        n_scr[...] = jnp.zeros(n_scr.shape, F32)
        m_scr[...] = jnp.zeros(m_scr.shape, F32)

    xs[pad:pad + L, 0:hw] = zq_ref[0]
    xs[pad:pad + L, hw:2 * hw] = zk_ref[0]
    y = bc_ref[...]
    for j in range(CONV_W):
        off = pad - (CONV_W - 1) + j
        y = y + xs[off:off + L, :] * wc_ref[j:j + 1, :]
    cq = y * jax.nn.sigmoid(y)
    xs[pad - (CONV_W - 1):pad, :] = xs[pad + L - (CONV_W - 1):pad + L, :]

    @pl.when(c_id == last)
    def _():
        conv_ref[0] = xs[pad - (CONV_W - 1):pad, :]

    g = gif_ref[0] + bif_ref[...]
    lf = _log_sigmoid(g)
    rows = lax.broadcasted_iota(jnp.int32, (L, L), 0)
    cols = lax.broadcasted_iota(jnp.int32, (L, L), 1)
    causal = rows >= cols
    tril = jnp.where(causal, 1.0, 0.0).astype(BF16)
    hi = lf.astype(BF16)
    r1 = lf - hi.astype(F32)
    mid = r1.astype(BF16)
    lo = (r1 - mid.astype(F32)).astype(BF16)
    bcum = _dot(tril, hi) + _dot(tril, mid) + _dot(tril, lo)
    bcum_t = bcum.T
    g_t = g.T

    for h in range(M_HEADS):
        hs = slice(h * M_DH, (h + 1) * M_DH)
        b_col = bcum[:, M_HEADS + h:M_HEADS + h + 1]
        b_row = bcum_t[M_HEADS + h:M_HEADS + h + 1, :]
        li_col = g[:, h:h + 1]
        li_row = g_t[h:h + 1, :]
        m_prev = m_scr[h][:, 0:1]
        log_d = jnp.where(causal, b_col - b_row + li_row, NEG)
        log_inter = b_col + m_prev
        m_t = jnp.maximum(log_inter, jnp.max(log_d, axis=1, keepdims=True))
        d = jnp.exp(log_d - m_t)
        w_inter = jnp.exp(log_inter - m_t)
        qh = cq[:, hs]
        kh = cq[:, hw + h * M_DH:hw + (h + 1) * M_DH] * (M_DH ** -0.5)
        vh = v_ref[0][:, hs]
        qb = qh.astype(BF16)
        s = _dot_nt(qb, kh.astype(BF16)) * d
        ct_h = ct[h]
        num = w_inter * _dot(qb, ct_h.astype(BF16)) + _dot(s.astype(BF16), vh.astype(BF16))
        n_prev = n_scr[h:h + 1, :]
        den = w_inter * jnp.sum(qh * n_prev, axis=1, keepdims=True) + jnp.sum(s, axis=1, keepdims=True)
        hh = num / jnp.maximum(jnp.abs(den), jnp.exp(-m_t))
        hm_ref[0, :, hs] = jax.nn.sigmoid(mo_ref[0][:, hs]) * _rms_rows(hh, gmh_ref[...])
        m_new = m_t[L - 1:L, :]
        b_last = b_col[L - 1:L, :]
        w_c = jnp.exp(b_last + m_prev - m_new)
        w_s = jnp.exp(b_last - b_col + li_col - m_new)
        ct[h] = w_c * ct_h + _dot(kh.T.astype(BF16), (w_s * vh).astype(BF16))
        n_scr[h:h + 1, :] = w_c * n_prev + jnp.sum(w_s * kh, axis=0, keepdims=True)
        m_scr[h] = jnp.broadcast_to(m_new, (1, LANES))

    @pl.when(c_id == last)
    def _():
        for h in range(M_HEADS):
            c_ref[0, h] = ct[h].T
            m_ref[0, h:h + 1, :] = m_scr[h]
        n_ref[0] = n_scr[0:M_HEADS, :]


def mlstm_prefill(z, gif, w_conv, b_conv, bif_row, g_mh, chunk=256):
    b, t, _ = z.shape
    chunk = min(chunk, t)
    assert t % chunk == 0 and t >= CONV_W - 1
    nc = t // chunk
    hw = M_HEADS * M_DH

    def zspec(c):
        return pl.BlockSpec((1, chunk, SEC), lambda bi, ci: (bi, ci, c - C_MQ))

    def full(shape):
        return pl.BlockSpec(shape, lambda bi, ci: (0,) * len(shape))

    return pl.pallas_call(
        functools.partial(_mlstm_prefill_kernel, chunk=chunk),
        grid=(b, nc),
        in_specs=[zspec(C_MQ), zspec(C_MK), zspec(C_MV), zspec(C_MO),
                  pl.BlockSpec((1, chunk, LANES), lambda bi, ci: (bi, ci, 0)),
                  full((CONV_W, 2 * hw)), full((1, 2 * hw)), full((1, LANES)), full((1, M_DH))],
        out_specs=[pl.BlockSpec((1, chunk, hw), lambda bi, ci: (bi, ci, 0)),
                   pl.BlockSpec((1, M_HEADS, M_DH, M_DH), lambda bi, ci: (bi, 0, 0, 0)),
                   pl.BlockSpec((1, M_HEADS, M_DH), lambda bi, ci: (bi, 0, 0)),
                   pl.BlockSpec((1, M_HEADS, LANES), lambda bi, ci: (bi, 0, 0)),
                   pl.BlockSpec((1, CONV_W - 1, 2 * hw), lambda bi, ci: (bi, 0, 0))],
        out_shape=[jax.ShapeDtypeStruct((b, t, hw), F32),
                   jax.ShapeDtypeStruct((b, M_HEADS, M_DH, M_DH), F32),
                   jax.ShapeDtypeStruct((b, M_HEADS, M_DH), F32),
                   jax.ShapeDtypeStruct((b, M_HEADS, LANES), F32),
                   jax.ShapeDtypeStruct((b, CONV_W - 1, 2 * hw), F32)],
        scratch_shapes=[pltpu.VMEM((chunk + SUBLANES, 2 * hw), F32),
                        pltpu.VMEM((M_HEADS, M_DH, M_DH), F32),
                        pltpu.VMEM((SUBLANES, M_DH), F32),
                        pltpu.VMEM((M_HEADS, 1, LANES), F32)],
        compiler_params=_cparams("parallel", "arbitrary"), name="mlstm_prefill",
    )(z, z, z, z, gif, w_conv, b_conv, bif_row, g_mh)


def _mlstm_decode_kernel(zq_ref, zk_ref, v_ref, mo_ref, gif_ref, buf_ref, c0_ref, n0_ref, m0_ref,
                         wc_ref, bc_ref, bif_ref, gmh_ref, hm_ref, c_ref, n_ref, m_ref, conv_ref):
    hw = M_HEADS * M_DH
    u = jnp.concatenate([zq_ref[0], zk_ref[0]], axis=1)
    buf = buf_ref[0]
    y = bc_ref[...]
    for j in range(CONV_W - 1):
        y = y + buf[j:j + 1, :] * wc_ref[j:j + 1, :]
    y = y + u * wc_ref[CONV_W - 1:CONV_W, :]
    cq = y * jax.nn.sigmoid(y)
    conv_ref[0] = jnp.concatenate([buf[1:CONV_W - 1, :], u], axis=0)

    g = gif_ref[0] + bif_ref[...]
    lf_all = _log_sigmoid(g)
    eye = (lax.broadcasted_iota(jnp.int32, (M_DH, M_DH), 0)
           == lax.broadcasted_iota(jnp.int32, (M_DH, M_DH), 1))
    v_all = v_ref[0]
    mo = mo_ref[0]
    m0 = m0_ref[0]
    outs, m_out = [], []
    for h in range(M_HEADS):
        hs = slice(h * M_DH, (h + 1) * M_DH)
        li = g[:, h:h + 1]
        lf = lf_all[:, M_HEADS + h:M_HEADS + h + 1]
        m_prev = m0[:, h:h + 1]
        log_inter = lf + m_prev
        m_t = jnp.maximum(log_inter, li)
        d = jnp.exp(li - m_t)
        w_inter = jnp.exp(log_inter - m_t)
        qh = cq[:, hs]
        kh = cq[:, hw + h * M_DH:hw + (h + 1) * M_DH] * (M_DH ** -0.5)
        vh = v_all[:, hs]
        s = jnp.sum(qh * kh, axis=1, keepdims=True) * d
        c_prev = c0_ref[0, h]
        cq_col = jnp.sum(c_prev * qh, axis=1, keepdims=True)
        v_col = jnp.sum(jnp.where(eye, vh, 0.0), axis=1, keepdims=True)
        num_col = w_inter * cq_col + s * v_col
        n_prev = n0_ref[0, h:h + 1, :]
        den = w_inter * jnp.sum(n_prev * qh, axis=1, keepdims=True) + s
        h_col = num_col / jnp.maximum(jnp.abs(den), jnp.exp(-m_t))
        h_row = jnp.sum(jnp.where(eye, h_col, 0.0), axis=0, keepdims=True)
        outs.append(jax.nn.sigmoid(mo[:, hs]) * _rms_rows(h_row, gmh_ref[...]))
        c_ref[0, h] = w_inter * c_prev + d * (v_col * kh)
        n_ref[0, h:h + 1, :] = w_inter * n_prev + d * kh
        m_out.append(m_t)
    hm_ref[0] = jnp.concatenate(outs, axis=1)
    lane = lax.broadcasted_iota(jnp.int32, (1, LANES), 1)
    m_row = jnp.zeros((1, LANES), F32)
    for h in range(M_HEADS):
        m_row = jnp.where(lane == h, m_out[h], m_row)
    m_ref[0] = m_row


def mlstm_decode(z, gif, buf, c0, n0, m0, w_conv, b_conv, bif_row, g_mh):
    b = z.shape[0]
    hw = M_HEADS * M_DH

    def zspec(c):
        return pl.BlockSpec((1, 1, SEC), lambda bi: (bi, 0, c))

    def per_b(shape):
        return pl.BlockSpec((1,) + shape, lambda bi: (bi,) + (0,) * len(shape))

    def full(shape):
        return pl.BlockSpec(shape, lambda bi: (0,) * len(shape))

    return pl.pallas_call(
        _mlstm_decode_kernel, grid=(b,),
        in_specs=[zspec(C_MQ), zspec(C_MK), zspec(C_MV), zspec(C_MO), per_b((1, LANES)),
                  per_b((CONV_W - 1, 2 * hw)), per_b((M_HEADS, M_DH, M_DH)), per_b((M_HEADS, M_DH)),
                  per_b((1, LANES)),
                  full((CONV_W, 2 * hw)), full((1, 2 * hw)), full((1, LANES)), full((1, M_DH))],
        out_specs=[per_b((1, hw)), per_b((M_HEADS, M_DH, M_DH)), per_b((M_HEADS, M_DH)),
                   per_b((1, LANES)), per_b((CONV_W - 1, 2 * hw))],
        out_shape=[jax.ShapeDtypeStruct((b, 1, hw), F32),
                   jax.ShapeDtypeStruct((b, M_HEADS, M_DH, M_DH), F32),
                   jax.ShapeDtypeStruct((b, M_HEADS, M_DH), F32),
                   jax.ShapeDtypeStruct((b, 1, LANES), F32),
                   jax.ShapeDtypeStruct((b, CONV_W - 1, 2 * hw), F32)],
        compiler_params=_cparams("parallel"), name="mlstm_decode",
    )(z, z, z, z, gif, buf, c0, n0, m0, w_conv, b_conv, bif_row, g_mh)


def _cross_attn_kernel(q_ref, k_ref, v_ref, o_ref):
    q = q_ref[0]
    tq = q.shape[0]
    if tq < SUBLANES:
        q = jnp.broadcast_to(q[0:1, :], (SUBLANES, q.shape[1]))
    k = k_ref[0].astype(BF16)
    v = v_ref[0].astype(BF16)
    for h in range(X_HEADS):
        hs = slice(h * X_DH, (h + 1) * X_DH)
        s = _dot_nt(q[:, hs].astype(BF16), k[:, hs]) * (X_DH ** -0.5)
        p = jnp.exp(s - jnp.max(s, axis=-1, keepdims=True))
        p = p / jnp.sum(p, axis=-1, keepdims=True)
        o = _dot(p.astype(BF16), v[:, hs])
        o_ref[0, :, hs] = o[0:tq, :]


def cross_attn(z, mk, mv, tq=512):
    b, t, zc = z.shape
    tq = min(tq, t)
    assert t % tq == 0
    n_mem = mk.shape[1]
    xq = zc // SEC - 1
    return pl.pallas_call(
        _cross_attn_kernel, grid=(b, t // tq),
        in_specs=[pl.BlockSpec((1, tq, SEC), lambda bi, qi: (bi, qi, xq)),
                  pl.BlockSpec((1, n_mem, SEC), lambda bi, qi: (bi, 0, 0)),
                  pl.BlockSpec((1, n_mem, SEC), lambda bi, qi: (bi, 0, 0))],
        out_specs=pl.BlockSpec((1, tq, SEC), lambda bi, qi: (bi, qi, 0)),
        out_shape=jax.ShapeDtypeStruct((b, t, SEC), F32),
        compiler_params=_cparams("parallel", "parallel"), name="cross_attn",
    )(z, mk, mv)


def _merge_kernel(x_ref, g_ref, oa_ref, hm_ref, ox_ref, wg_ref, wb_ref, wo_ref, o_ref):
    x = x_ref[...]
    d = x.shape[1]
    h = _rms_rows(x, g_ref[...]).astype(BF16)
    merged = None
    for i, br in enumerate((oa_ref, hm_ref, ox_ref)):
        gate = jax.nn.sigmoid(_dot(h, wg_ref[:, i * d:(i + 1) * d]))
        term = gate * _dot(br[...].astype(BF16), wb_ref[i])
        merged = term if merged is None else merged + term
    o_ref[...] = x + _dot(merged.astype(BF16), wo_ref[...])


def merge(x, g, oa, hm, ox, w_gate, w_branch, w_o, tm=256):
    t, d = x.shape
    tm = min(tm, t)
    assert t % tm == 0

    def rows(width):
        return pl.BlockSpec((tm, width), lambda i: (i, 0))

    def full(shape):
        return pl.BlockSpec(shape, lambda i: (0,) * len(shape))

    return pl.pallas_call(
        _merge_kernel, grid=(t // tm,),
        in_specs=[rows(d), full((1, d)), rows(SEC), rows(SEC), rows(SEC),
                  full(w_gate.shape), full(w_branch.shape), full(w_o.shape)],
        out_specs=rows(d), out_shape=jax.ShapeDtypeStruct((t, d), F32),
        compiler_params=_cparams("parallel"), name="merge",
    )(x, g, oa, hm, ox, w_gate, w_branch, w_o)


def _route_top2(h, wr_hi, wr_lo, b_router):
    h_hi = h.astype(BF16)
    h_lo = (h - h_hi.astype(F32)).astype(BF16)
    logits = _dot(h_hi, wr_hi) + (_dot(h_hi, wr_lo) + _dot(h_lo, wr_hi)) + b_router
    lane = lax.broadcasted_iota(jnp.int32, logits.shape, 1)
    logits = jnp.where(lane < N_EXPERTS, logits, -jnp.inf)
    v1 = jnp.max(logits, axis=-1, keepdims=True)
    i1 = jnp.min(jnp.where(logits == v1, lane, LANES), axis=-1, keepdims=True)
    rest = jnp.where(lane == i1, -jnp.inf, logits)
    v2 = jnp.max(rest, axis=-1, keepdims=True)
    i2 = jnp.min(jnp.where(rest == v2, lane, LANES), axis=-1, keepdims=True)
    e2 = jnp.exp(v2 - v1)
    den = 1.0 + e2
    gates = jnp.where(lane == i1, 1.0 / den, 0.0) + jnp.where(lane == i2, e2 / den, 0.0)
    return gates, jnp.logical_or(lane == i1, lane == i2)


def _swiglu(h, wg, wu, wd):
    gg = _dot(h, wg)
    uu = _dot(h, wu)
    return _dot(((gg * jax.nn.sigmoid(gg)) * uu).astype(BF16), wd)


def _ffn_kernel(*refs, routed, blk):
    if routed and blk:
        (x_ref, g_ref, wg_ref, wu_ref, wd_ref, wrh_ref, wrl_ref, br_ref, o_ref,
         h_scr, gate_scr, key_scr, keyt_scr, cnt_scr) = refs
    elif routed:
        x_ref, g_ref, wg_ref, wu_ref, wd_ref, wrh_ref, wrl_ref, br_ref, o_ref, h_scr, gate_scr = refs
    else:
        x_ref, g_ref, wg_ref, wu_ref, wd_ref, o_ref, h_scr = refs
    c = pl.program_id(1)
    tm = x_ref.shape[0]

    @pl.when(c == 0)
    def _():
        x = x_ref[...]
        h = _rms_rows(x, g_ref[...])
        h_scr[...] = h.astype(BF16)
        o_ref[...] = x
        if routed:
            gates, sel = _route_top2(h, wrh_ref[...], wrl_ref[...], br_ref[...])
            gate_scr[...] = gates
        if routed and blk:
            earlier = (lax.broadcasted_iota(jnp.int32, (tm, tm), 1)
                       < lax.broadcasted_iota(jnp.int32, (tm, tm), 0))
            sel01 = jnp.where(sel, 1.0, 0.0)
            rank = _dot(jnp.where(earlier, 1.0, 0.0).astype(BF16), sel01.astype(BF16))
            key = jnp.where(sel, rank, -1.0)
            key_scr[...] = key
            keyt_scr[...] = key.T
            cnt_scr[...] = jnp.sum(sel01, axis=0, keepdims=True)

    if not routed:
        o_ref[...] += _swiglu(h_scr[...], wg_ref[0], wu_ref[0], wd_ref[0])
        return
    lane = lax.broadcasted_iota(jnp.int32, gate_scr.shape, 1)
    gate_col = jnp.sum(jnp.where(lane == c, gate_scr[...], 0.0), axis=-1, keepdims=True)
    if not blk:
        o_ref[...] += gate_col * _swiglu(h_scr[...], wg_ref[0], wu_ref[0], wd_ref[0])
        return

    key_col = jnp.sum(jnp.where(lane == c, key_scr[...], 0.0), axis=-1, keepdims=True)
    key_row = keyt_scr[pl.ds(c, 1), :]
    lane1 = lax.broadcasted_iota(jnp.int32, cnt_scr.shape, 1)
    count = jnp.sum(jnp.where(lane1 == c, cnt_scr[...], 0.0)).astype(jnp.int32)
    half = tm // 2

    def run_block(first_rank, size):
        base = first_rank.astype(F32)
        slot_rows = lax.broadcasted_iota(jnp.int32, (size, tm), 0).astype(F32)
        slot_cols = lax.broadcasted_iota(jnp.int32, (tm, size), 1).astype(F32)
        pick = jnp.where(key_row - base == slot_rows, 1.0, 0.0).astype(BF16)
        h_sel = _dot(pick, h_scr[...]).astype(BF16)
        y = _swiglu(h_sel, wg_ref[0], wu_ref[0], wd_ref[0]).astype(BF16)
        place = jnp.where(key_col - base == slot_cols, 1.0, 0.0).astype(BF16)
        for r in range(2):
            rows = slice(r * half, (r + 1) * half)
            o_ref[rows, :] += gate_col[rows, :] * _dot(place[rows, :], y)

    small = blk // 2
    n_full = count // blk
    rem = count - n_full * blk

    def block(b, carry):
        run_block(b * blk, blk)
        return carry

    lax.fori_loop(0, n_full + (rem > small).astype(jnp.int32), block, 0)

    @pl.when(jnp.logical_and(rem > 0, rem <= small))
    def _():
        run_block(n_full * blk, small)


def ffn(x, g, w_gate, w_up, w_down, router=None, tm=512, blk=256):
    t, d = x.shape
    if router is not None:
        tm = 2 * tm
    tm = min(tm, t)
    assert t % tm == 0
    blk = blk if (router is not None and tm >= 2 * blk) else 0
    wg_arr, wg_map = w_gate
    wu_arr, wu_map = w_up
    n_c, f, _ = w_down.shape
    in_specs = [pl.BlockSpec((tm, d), lambda i, c: (i, 0)),
                pl.BlockSpec((1, d), lambda i, c: (0, 0)),
                pl.BlockSpec((1, d, f), lambda i, c: wg_map(c)),
                pl.BlockSpec((1, d, f), lambda i, c: wu_map(c)),
                pl.BlockSpec((1, f, d), lambda i, c: (c, 0, 0))]
    args = [x, g, wg_arr, wu_arr, w_down]
    scratch = [pltpu.VMEM((tm, d), BF16)]
    if router is not None:
        in_specs += [pl.BlockSpec((d, LANES), lambda i, c: (0, 0)),
                     pl.BlockSpec((d, LANES), lambda i, c: (0, 0)),
                     pl.BlockSpec((1, LANES), lambda i, c: (0, 0))]
        args += list(router)
        scratch.append(pltpu.VMEM((tm, LANES), F32))
        if blk:
            scratch += [pltpu.VMEM((tm, LANES), F32), pltpu.VMEM((LANES, tm), F32), pltpu.VMEM((1, LANES), F32)]
    return pl.pallas_call(
        functools.partial(_ffn_kernel, routed=router is not None, blk=blk), grid=(t // tm, n_c),
        in_specs=in_specs, out_specs=pl.BlockSpec((tm, d), lambda i, c: (i, 0)),
        out_shape=jax.ShapeDtypeStruct((t, d), F32), scratch_shapes=scratch,
        compiler_params=_cparams("parallel", "arbitrary"), name="ffn")(*args)


def _pad_lanes(a):
    return jnp.pad(a, ((0, 0),) * (a.ndim - 1) + ((0, LANES - a.shape[-1]),))


def kernel(x_prompt, x_sample, mem_prompt, cache_k, cache_v, page_table, cache_mem_k, cache_mem_v, state_C, state_n, state_m, state_conv, g_attn_norm, w_in, b_if, g_q, g_k, lam_q1, lam_k1, lam_q2, lam_k2, g_subln, w_conv, b_conv, g_mh, g_mem_norm, w_mem_kv, g_mq, g_mk, w_branch, w_o, g_ffn_norm, w_dense_gu, w_dense_down, w_router, b_router, w_moe_gu, w_moe_down):
    depth = w_in.shape[0]
    bp, tp, d = x_prompt.shape
    bs, ts, _ = x_sample.shape
    assert ts == 1, "the sample group decodes one token per sequence"
    n_mem = mem_prompt.shape[1]
    hw = M_HEADS * M_DH
    n_pool, page = cache_k.shape[1], cache_k.shape[2]
    cache_kt = jnp.transpose(cache_k, (0, 1, 3, 4, 5, 2)).reshape(depth, n_pool, SEC, page)
    cache_v2 = cache_v.reshape(depth, n_pool, page * A_HEADS, A_DV)

    o_aq, o_ak, o_av, o_mqk = 0, 512, 1024, 1536
    o_mv, o_mo, o_mif, o_xq, o_g = 2560, 3072, 3584, 3592, 4104

    yp = x_prompt.reshape(bp * tp, d)
    ys = x_sample.reshape(bs * ts, d)
    mem = mem_prompt.reshape(bp * n_mem, d)
    outs = {k: [] for k in ("mkp", "mvp", "cp", "np", "mp", "convp", "ks", "vs", "cs", "ns", "ms", "convs")}
    k_state = v_state = None
    ones = jnp.ones((SEC,), F32)
    for l in range(depth):
        lam_init = 0.8 - 0.6 * math.exp(-0.3 * l)
        lam = (jnp.exp(jnp.sum(lam_q1[l] * lam_k1[l])) - jnp.exp(jnp.sum(lam_q2[l] * lam_k2[l]))
               + lam_init).reshape(1).astype(F32)
        wl = w_in[l]
        w_main = jnp.concatenate([wl[:, o_aq:o_mif], wl[:, o_xq:o_g]], axis=1).astype(BF16)
        w_if = _pad_lanes(wl[:, o_mif:o_xq]).astype(BF16)
        w_gate = wl[:, o_g:].astype(BF16)
        q_scale = (A_DQK ** -0.5) * math.log2(math.e)
        gains = jnp.concatenate([jnp.tile(g_q[l], SEC // A_DQK) * q_scale, jnp.tile(g_k[l], SEC // A_DQK),
                                 ones, ones, ones, ones, ones, jnp.tile(g_mq[l], SEC // X_DH)])[None, :]
        modes = (A_DQK, A_DQK, 0, 0, 0, 0, 0, X_DH)
        g_attn = g_attn_norm[l][None, :]
        bif_row = _pad_lanes(b_if[l][None, :])
        wc, bc, gmh = w_conv[l], b_conv[l][None, :], g_mh[l][None, :]
        wb, wo = w_branch[l].astype(BF16), w_o[l].astype(BF16)
        out_scale = 1.0 - lam_init

        kv_gains = jnp.concatenate([jnp.tile(g_mk[l], SEC // X_DH), ones])[None, :]
        mkv, = norm_proj(mem, g_mem_norm[l][None, :], w_mem_kv[l].astype(BF16), kv_gains, (X_DH, 0))
        mk_p = mkv[:, :SEC].reshape(bp, n_mem, SEC)
        mv_p = mkv[:, SEC:].reshape(bp, n_mem, SEC)

        z, gif, qt, kb, vt, k_state, v_state = norm_proj(
            yp, g_attn, w_main, gains, modes, w_extra=w_if,
            prefill=(bp, tp, min(512, tp), l, depth, k_state, v_state))
        z3 = z.reshape(bp, tp, z.shape[1])
        oa = attn_prefill(qt, kb, vt, lam, g_subln[l], out_scale)
        hm, c_p, n_p, m_p, conv_p = mlstm_prefill(z3, gif.reshape(bp, tp, LANES), wc, bc, bif_row, gmh)
        ox = cross_attn(z3, mk_p, mv_p)
        yp = merge(yp, g_attn, oa.reshape(bp * tp, SEC), hm.reshape(bp * tp, hw), ox.reshape(bp * tp, SEC),
                   w_gate, wb, wo)

        zs, gifs = norm_proj(ys, g_attn, w_main, gains, modes, w_extra=w_if)
        zs3 = zs.reshape(bs, 1, Z_COLS)
        oa_s = attn_decode(zs3, cache_kt, cache_v2, page_table, l, lam, g_subln[l], out_scale)
        hm_s, c_s, n_s, m_s, conv_s = mlstm_decode(
            zs3, gifs.reshape(bs, 1, LANES), state_conv[l], state_C[l], state_n[l],
            _pad_lanes(state_m[l])[:, None, :], wc, bc, bif_row, gmh)
        ox_s = cross_attn(zs3, cache_mem_k[l].reshape(bs, n_mem, SEC), cache_mem_v[l].reshape(bs, n_mem, SEC))
        ys = merge(ys, g_attn, oa_s.reshape(bs, SEC), hm_s.reshape(bs, hw), ox_s.reshape(bs, SEC),
                   w_gate, wb, wo)
        ka_s = zs3[:, :, C_AK * SEC:(C_AK + 1) * SEC].reshape(bs, 1, A_HEADS, 2, A_DQK)
        va_s = zs3[:, :, C_AV * SEC:(C_AV + 1) * SEC].reshape(bs, 1, A_HEADS, A_DV)

        i = l // 2
        g_ffn = g_ffn_norm[l][None, :]
        if l % 2 == 0:
            wgu = w_dense_gu[i].astype(BF16)[None]
            d_ff = w_dense_down.shape[1]
            n_c = 2 if d_ff % (2 * LANES) == 0 else 1
            f = d_ff // n_c
            w_g = (wgu, lambda c: (0, 0, c))
            w_u = (wgu, lambda c, n_c=n_c: (0, 0, n_c + c))
            w_d = w_dense_down[i].astype(BF16).reshape(n_c, f, d)
            yp = ffn(yp, g_ffn, w_g, w_u, w_d)
            ys = ffn(ys, g_ffn, w_g, w_u, w_d)
        else:
            wgu = w_moe_gu[i].astype(BF16)
            w_g = (wgu, lambda c: (c, 0, 0))
            w_u = (wgu, lambda c: (c, 0, 1))
            w_d = w_moe_down[i].astype(BF16)
            wr = _pad_lanes(w_router[i])
            wr_hi = wr.astype(BF16)
            wr_lo = (wr - wr_hi.astype(F32)).astype(BF16)
            router = (wr_hi, wr_lo, _pad_lanes(b_router[i][None, :]))
            yp = ffn(yp, g_ffn, w_g, w_u, w_d, router)
            ys = ffn(ys, g_ffn, w_g, w_u, w_d, router)

        for name, val in zip(outs, (mk_p.reshape(bp, n_mem, X_HEADS, X_DH),
                                    mv_p.reshape(bp, n_mem, X_HEADS, X_DH), c_p, n_p, m_p[:, :, 0], conv_p,
                                    ka_s, va_s, c_s, n_s, m_s[:, 0, :M_HEADS], conv_s)):
            outs[name].append(val)

    st = {name: jnp.stack(vals, axis=0) for name, vals in outs.items()}
    st["kp"] = jnp.transpose(k_state.reshape(depth, bp, A_HEADS, 2, A_DQK, tp), (0, 1, 5, 2, 3, 4))
    st["vp"] = v_state.reshape(depth, bp, tp, A_HEADS, A_DV)
    return (yp.reshape(bp, tp, d), ys.reshape(bs, ts, d), st["kp"], st["vp"], st["mkp"], st["mvp"],
            st["cp"], st["np"], st["mp"], st["convp"], st["ks"], st["vs"], st["cs"], st["ns"], st["ms"],
            st["convs"])
```

```python
import functools
import math

import jax
import jax.numpy as jnp
from jax import lax
from jax.experimental import pallas as pl
from jax.experimental.pallas import tpu as pltpu

F32 = jnp.float32
BF16 = jnp.bfloat16

EPS = 1e-6
NEG = -1e30

A_HEADS = 4
A_DQK = 64
A_DV = 128
M_HEADS = 4
M_DH = 128
CONV_W = 4
X_HEADS = 4
X_DH = 128
N_BRANCH = 3
N_EXPERTS = 8
LANES = 128
SUBLANES = 8

SEC = 512
C_AQ, C_AK, C_AV, C_MQ, C_MK, C_MV, C_MO, C_XQ = range(8)
Z_COLS = 8 * SEC

VMEM_LIMIT = 56 * 1024 * 1024


def _cparams(*sem):
    return pltpu.CompilerParams(dimension_semantics=sem, vmem_limit_bytes=VMEM_LIMIT)


def _dot(a, b):
    return jnp.dot(a, b, preferred_element_type=F32)


def _dot_nt(a, b):
    return lax.dot_general(a, b, (((1,), (1,)), ((), ())), preferred_element_type=F32)


def _rms_rows(x, g):
    return x * lax.rsqrt(jnp.mean(x * x, axis=-1, keepdims=True) + EPS) * g


def _log_sigmoid(x):
    return jnp.minimum(x, 0.0) - jnp.log1p(jnp.exp(-jnp.abs(x)))


def _group_norm_cols(zs, gains, group):
    if group == 0:
        return zs
    outs = []
    for i in range(zs.shape[1] // LANES):
        zb = zs[:, i * LANES:(i + 1) * LANES]
        sq = zb * zb
        if group == LANES:
            ms = jnp.mean(sq, axis=-1, keepdims=True)
        else:
            lane = lax.broadcasted_iota(jnp.int32, zb.shape, 1)
            low = lane < group
            s_lo = jnp.sum(jnp.where(low, sq, 0.0), axis=-1, keepdims=True)
            s_hi = jnp.sum(jnp.where(low, 0.0, sq), axis=-1, keepdims=True)
            ms = jnp.where(low, s_lo, s_hi) * (1.0 / group)
        outs.append(zb * lax.rsqrt(ms + EPS) * gains[:, i * LANES:(i + 1) * LANES])
    return jnp.concatenate(outs, axis=1)


def _norm_proj_kernel(*refs, modes, has_extra, n_alias, tkv):
    x_ref, g_ref, w_ref, gains_ref = refs[:4]
    pos = 4
    if has_extra:
        wx_ref = refs[pos]
        pos += 1
    pos += n_alias
    o_ref = refs[pos]
    pos += 1
    if has_extra:
        ox_ref = refs[pos]
        pos += 1
    if tkv:
        qt_ref, kb_ref, vt_ref, ko_ref, vo_ref = refs[pos:pos + 5]
    tm = x_ref.shape[0]
    first = 3 if tkv else 0
    h = _rms_rows(x_ref[...], g_ref[...]).astype(BF16)
    if has_extra:
        ox_ref[...] = _dot(h, wx_ref[...])
    for s, mode in enumerate(modes):
        cs = slice(s * SEC, (s + 1) * SEC)
        acc = _group_norm_cols(_dot(h, w_ref[:, cs]), gains_ref[:, cs], mode)
        if s >= first:
            o_ref[:, (s - first) * SEC:(s - first + 1) * SEC] = acc
            continue
        for hh in range(A_HEADS):
            a_h = acc[:, hh * LANES:(hh + 1) * LANES]
            if s == 0:
                qt_ref[0, hh] = a_h.T.astype(BF16)
            elif s == 1:
                ko_ref[0, 0, hh * LANES:(hh + 1) * LANES, :] = a_h.T
                for tt in range(tm // tkv):
                    kb_ref[0, hh, tt] = a_h[tt * tkv:(tt + 1) * tkv, :].astype(BF16)
            else:
                vo_ref[0, 0, pl.ds(hh, tm, stride=A_HEADS), :] = a_h
                for tt in range(tm // tkv):
                    vt_ref[0, hh, tt] = a_h[tt * tkv:(tt + 1) * tkv, :].T.astype(BF16)


def norm_proj(x, g, w, gains, modes, w_extra=None, tm=512, prefill=None):
    t, d = x.shape
    n = w.shape[1] // SEC
    tm = min(tm, t)
    assert t % tm == 0 and len(modes) == n

    def full(shape):
        return pl.BlockSpec(shape, lambda i: (0,) * len(shape))

    in_specs = [pl.BlockSpec((tm, d), lambda i: (i, 0)), full((1, d)), full((d, n * SEC)), full((1, n * SEC))]
    args = [x, g, w, gains]
    if w_extra is not None:
        in_specs.append(full((d, LANES)))
        args.append(w_extra)
    first, tkv, aliases = 0, 0, {}
    if prefill is not None:
        bsz, seq, tkv, layer, depth, k_state, v_state = prefill
        assert seq % tm == 0 and tm % tkv == 0 and bsz * seq == t
        first, nt = 3, seq // tm
    out_shape = [jax.ShapeDtypeStruct((t, (n - first) * SEC), F32)]
    out_specs = [pl.BlockSpec((tm, (n - first) * SEC), lambda i: (i, 0))]
    if w_extra is not None:
        out_shape.append(jax.ShapeDtypeStruct((t, LANES), F32))
        out_specs.append(pl.BlockSpec((tm, LANES), lambda i: (i, 0)))
    if prefill is not None:
        kpt = tm // tkv
        out_shape += [jax.ShapeDtypeStruct((bsz, A_HEADS, LANES, seq), BF16),
                      jax.ShapeDtypeStruct((bsz, A_HEADS, seq // tkv, tkv, LANES), BF16),
                      jax.ShapeDtypeStruct((bsz, A_HEADS, seq // tkv, LANES, tkv), BF16),
                      jax.ShapeDtypeStruct((depth, bsz, SEC, seq), F32),
                      jax.ShapeDtypeStruct((depth, bsz, seq * A_HEADS, A_DV), F32)]
        out_specs += [pl.BlockSpec((1, A_HEADS, LANES, tm), lambda i: (i // nt, 0, 0, i % nt)),
                      pl.BlockSpec((1, A_HEADS, kpt, tkv, LANES), lambda i: (i // nt, 0, i % nt, 0, 0)),
                      pl.BlockSpec((1, A_HEADS, kpt, LANES, tkv), lambda i: (i // nt, 0, i % nt, 0, 0)),
                      pl.BlockSpec((1, 1, SEC, tm), lambda i: (layer, i // nt, 0, i % nt)),
                      pl.BlockSpec((1, 1, tm * A_HEADS, A_DV), lambda i: (layer, i // nt, i % nt, 0))]
        if k_state is not None:
            n_out = len(out_shape)
            aliases = {len(args): n_out - 2, len(args) + 1: n_out - 1}
            in_specs += [pl.BlockSpec(memory_space=pl.ANY)] * 2
            args += [k_state, v_state]
    return pl.pallas_call(
        functools.partial(_norm_proj_kernel, modes=tuple(modes), has_extra=w_extra is not None,
                          n_alias=len(aliases), tkv=tkv),
        grid=(t // tm,), in_specs=in_specs, out_specs=out_specs, out_shape=out_shape,
        input_output_aliases=aliases,
        compiler_params=_cparams("parallel"), name="norm_proj")(*args)


def _attn_prefill_kernel(lam_ref, qt_ref, kb_ref, vt_ref, gs_ref, o_ref, s_scr, p_scr, a_scr,
                         m_scr, l_scr, acc_scr, *, tq, out_scale):
    qi = pl.program_id(2)
    kb = kb_ref.at[0, 0]
    vt = vt_ref.at[0, 0]
    qt = qt_ref[0, 0]
    drow = lax.broadcasted_iota(jnp.int32, qt.shape, 0)
    zero = jnp.zeros_like(qt)
    qq = jnp.concatenate([jnp.where(drow < A_DQK, qt, zero),
                          jnp.where(drow >= A_DQK, qt, zero)], axis=1)

    m_scr[...] = jnp.full(m_scr.shape, NEG, F32)
    l_scr[...] = jnp.zeros(l_scr.shape, F32)
    acc_scr[...] = jnp.zeros(acc_scr.shape, F32)
    p_scr[1] = jnp.zeros(p_scr.shape[1:], BF16)
    a_scr[1] = jnp.ones(a_scr.shape[1:], F32)

    def scores(j):
        return _dot(kb[j], qq)

    def softmax_stage(slot):
        s = s_scr[slot]
        m_old = m_scr[...]
        m_new = jnp.maximum(m_old, jnp.max(s, axis=0, keepdims=True))
        alpha = jnp.exp2(m_old - m_new)
        p = jnp.exp2(s - m_new)
        l_scr[...] = alpha * l_scr[...] + jnp.sum(p, axis=0, keepdims=True)
        m_scr[...] = m_new
        p_scr[slot] = p.astype(BF16)
        a_scr[slot] = alpha

    def accumulate_stage(slot, jv):
        acc_scr[...] = a_scr[slot] * acc_scr[...] + _dot(vt[jv], p_scr[slot])

    def kv_of_visit(k):
        return jnp.where(k <= 0, qi, k - 1)

    s = scores(qi)
    kpos = lax.broadcasted_iota(jnp.int32, s.shape, 0)
    qpos = lax.broadcasted_iota(jnp.int32, s.shape, 1) % tq
    s_scr[0] = jnp.where(kpos <= qpos, s, NEG)

    def body(j, c):
        slot = j % 2
        accumulate_stage(1 - slot, kv_of_visit(j - 1))
        s_next = scores(j)
        softmax_stage(slot)
        s_scr[1 - slot] = s_next
        return c

    lax.fori_loop(0, qi, body, 0)
    last = qi % 2
    accumulate_stage(1 - last, kv_of_visit(qi - 1))
    softmax_stage(last)
    accumulate_stage(last, kv_of_visit(qi))

    inv = 1.0 / l_scr[...]
    acc = acc_scr[...]
    o = acc[:, :tq] * inv[:, :tq] - lam_ref[0] * (acc[:, tq:] * inv[:, tq:])
    o = o * lax.rsqrt(jnp.mean(o * o, axis=0, keepdims=True) + EPS) * gs_ref[...] * out_scale
    o_ref[0] = o.T


def attn_prefill(qt, kb, vt, lam, g_subln, out_scale):
    b, _, nq, tq, _ = kb.shape
    t = nq * tq

    return pl.pallas_call(
        functools.partial(_attn_prefill_kernel, tq=tq, out_scale=out_scale),
        grid=(b, A_HEADS, nq),
        in_specs=[pl.BlockSpec(memory_space=pltpu.SMEM),
                  pl.BlockSpec((1, 1, LANES, tq), lambda bi, h, qi: (bi, h, 0, qi)),
                  pl.BlockSpec((1, 1, nq, tq, LANES), lambda bi, h, qi: (bi, h, 0, 0, 0)),
                  pl.BlockSpec((1, 1, nq, LANES, tq), lambda bi, h, qi: (bi, h, 0, 0, 0)),
                  pl.BlockSpec((A_DV, 1), lambda bi, h, qi: (0, 0))],
        out_specs=pl.BlockSpec((1, tq, LANES), lambda bi, h, qi: (bi, qi, h)),
        out_shape=jax.ShapeDtypeStruct((b, t, SEC), F32),
        scratch_shapes=[pltpu.VMEM((2, tq, 2 * tq), F32), pltpu.VMEM((2, tq, 2 * tq), BF16),
                        pltpu.VMEM((2, 1, 2 * tq), F32),
                        pltpu.VMEM((1, 2 * tq), F32), pltpu.VMEM((1, 2 * tq), F32),
                        pltpu.VMEM((A_DV, 2 * tq), F32)],
        compiler_params=_cparams("parallel", "parallel", "arbitrary"), name="attn_prefill",
    )(lam, qt, kb, vt, g_subln.reshape(A_DV, 1))


def _attn_decode_kernel(pt_ref, lam_ref, q_ref, kn_ref, vn_ref, gs_ref, *refs, pp, out_scale):
    k_refs, v_refs = refs[:pp], refs[pp:2 * pp]
    o_ref, qcol_scr, p_scr, a_scr, m_scr, l_scr, acc_scr = refs[2 * pp:]
    s_id = pl.program_id(1)
    n_steps = pl.num_programs(1) - 1
    n_rows = 2 * A_HEADS
    page = k_refs[0].shape[1]
    row1 = lax.broadcasted_iota(jnp.int32, (n_rows, 1), 0)

    @pl.when(s_id == 0)
    def _():
        eye = (lax.broadcasted_iota(jnp.int32, (LANES, LANES), 0)
               == lax.broadcasted_iota(jnp.int32, (LANES, LANES), 1))
        for blk in range(SEC // LANES):
            qb = q_ref[0][:, blk * LANES:(blk + 1) * LANES]
            q_col = jnp.sum(jnp.where(eye, qb, 0.0), axis=1, keepdims=True)
            qcol_scr[blk * LANES:(blk + 1) * LANES, :] = jnp.broadcast_to(q_col, (LANES, page))
        m_scr[...] = jnp.full(m_scr.shape, NEG, F32)
        l_scr[...] = jnp.zeros(l_scr.shape, F32)
        acc_scr[...] = jnp.zeros(acc_scr.shape, F32)
        p_scr[...] = jnp.zeros(p_scr.shape, BF16)
        a_scr[...] = jnp.ones(a_scr.shape, F32)

    pb = p_scr[...]
    pv = jnp.zeros((n_rows, A_DV), F32)
    for i in range(pp):
        p_i = pb[:, i * page:(i + 1) * page]
        for h in range(A_HEADS):
            v_h = v_refs[i][pl.ds(h, page, stride=A_HEADS), :]
            pv = pv + jnp.where(row1 // 2 == h, _dot(p_i, v_h.astype(BF16)), 0.0)
    acc_scr[...] = a_scr[...] * acc_scr[...] + pv

    qcol = qcol_scr[...]
    s_pages = []
    for i in range(pp):
        prod = k_refs[i][...] * qcol
        s_pages.append(jnp.concatenate(
            [jnp.sum(prod[r * A_DQK:(r + 1) * A_DQK, :], axis=0, keepdims=True) for r in range(n_rows)], axis=0))
    s = jnp.concatenate(s_pages, axis=1)
    s = jnp.where(s_id < n_steps, s, NEG)
    m_old = m_scr[...]
    m_new = jnp.maximum(m_old, jnp.max(s, axis=1, keepdims=True))
    alpha = jnp.exp2(m_old - m_new)
    p = jnp.exp2(s - m_new)
    l_scr[...] = alpha * l_scr[...] + jnp.sum(p, axis=1, keepdims=True)
    m_scr[...] = m_new
    p_scr[...] = p.astype(BF16)
    a_scr[...] = alpha

    @pl.when(s_id == n_steps)
    def _():
        row = lax.broadcasted_iota(jnp.int32, (n_rows, SEC), 0)
        col = lax.broadcasted_iota(jnp.int32, (n_rows, SEC), 1)
        qm = jnp.where(col // A_DQK == row, q_ref[0], 0.0)
        s_self = jnp.sum(qm * kn_ref[0], axis=1, keepdims=True)
        vn = vn_ref[0]
        vn_rows = jnp.concatenate([vn[:, (r // 2) * A_DV:(r // 2 + 1) * A_DV] for r in range(n_rows)], axis=0)
        m_old = m_scr[...]
        m_fin = jnp.maximum(m_old, s_self)
        alpha = jnp.exp2(m_old - m_fin)
        p_self = jnp.exp2(s_self - m_fin)
        l_fin = alpha * l_scr[...] + p_self
        acc = alpha * acc_scr[...] + p_self * vn_rows
        t = acc * (jnp.where(row1 % 2 == 0, 1.0, -lam_ref[0]) / l_fin)
        outs = []
        for h in range(A_HEADS):
            oh = t[2 * h:2 * h + 1, :] + t[2 * h + 1:2 * h + 2, :]
            outs.append(_rms_rows(oh, gs_ref[...]) * out_scale)
        o_ref[0] = jnp.concatenate(outs, axis=1)


def attn_decode(z, cache_kt, cache_v, page_table, layer, lam, g_subln, out_scale, pp=16):
    b = z.shape[0]
    n_pages = page_table.shape[1]
    page = cache_kt.shape[3]
    while n_pages % pp:
        pp //= 2

    n_steps = n_pages // pp

    def page_spec(i, shape, lag):
        def index(bi, s, pt):
            step = jnp.clip(s - lag, 0, n_steps - 1)
            return (layer, pt[bi, step * pp + i], 0, 0)
        return pl.BlockSpec((None, None) + shape, index)

    def row_spec(c):
        return pl.BlockSpec((1, 1, SEC), lambda bi, s, pt: (bi, 0, c))

    grid_spec = pltpu.PrefetchScalarGridSpec(
        num_scalar_prefetch=1, grid=(b, n_steps + 1),
        in_specs=[pl.BlockSpec(memory_space=pltpu.SMEM), row_spec(C_AQ), row_spec(C_AK), row_spec(C_AV),
                  pl.BlockSpec((1, A_DV), lambda bi, s, pt: (0, 0))]
                 + [page_spec(i, (SEC, page), 0) for i in range(pp)]
                 + [page_spec(i, (page * A_HEADS, A_DV), 1) for i in range(pp)],
        out_specs=pl.BlockSpec((1, 1, SEC), lambda bi, s, pt: (bi, 0, 0)),
        scratch_shapes=[pltpu.VMEM((SEC, page), F32),
                        pltpu.VMEM((2 * A_HEADS, pp * page), BF16), pltpu.VMEM((2 * A_HEADS, 1), F32),
                        pltpu.VMEM((2 * A_HEADS, 1), F32), pltpu.VMEM((2 * A_HEADS, 1), F32),
                        pltpu.VMEM((2 * A_HEADS, A_DV), F32)])
    return pl.pallas_call(
        functools.partial(_attn_decode_kernel, pp=pp, out_scale=out_scale),
        grid_spec=grid_spec, out_shape=jax.ShapeDtypeStruct((b, 1, SEC), F32),
        compiler_params=_cparams("parallel", "arbitrary"), name="attn_decode",
    )(page_table, lam, z, z, z, g_subln.reshape(1, A_DV), *([cache_kt] * pp), *([cache_v] * pp))


def _mlstm_prefill_kernel(zq_ref, zk_ref, v_ref, mo_ref, gif_ref, wc_ref, bc_ref, bif_ref, gmh_ref,
                          hm_ref, c_ref, n_ref, m_ref, conv_ref, xs, ct, n_scr, m_scr, *, chunk):
    c_id = pl.program_id(1)
    last = pl.num_programs(1) - 1
    L = chunk
    hw = M_HEADS * M_DH
    pad = SUBLANES

    @pl.when(c_id == 0)
    def _():
        xs[0:pad, :] = jnp.zeros((pad, 2 * hw), F32)
        ct[...] = jnp.zeros(ct.shape, F32)
        n_scr[...] = jnp.zeros(n_scr.shape, F32)
        m_scr[...] = jnp.zeros(m_scr.shape, F32)

    xs[pad:pad + L, 0:hw] = zq_ref[0]
    xs[pad:pad + L, hw:2 * hw] = zk_ref[0]
    y = bc_ref[...]
    for j in range(CONV_W):
        off = pad - (CONV_W - 1) + j
        y = y + xs[off:off + L, :] * wc_ref[j:j + 1, :]
    cq = y * jax.nn.sigmoid(y)
    xs[pad - (CONV_W - 1):pad, :] = xs[pad + L - (CONV_W - 1):pad + L, :]

    @pl.when(c_id == last)
    def _():
        conv_ref[0] = xs[pad - (CONV_W - 1):pad, :]

    g = gif_ref[0] + bif_ref[...]
    lf = _log_sigmoid(g)
    rows = lax.broadcasted_iota(jnp.int32, (L, L), 0)
    cols = lax.broadcasted_iota(jnp.int32, (L, L), 1)
    causal = rows >= cols
    tril = jnp.where(causal, 1.0, 0.0).astype(BF16)
    hi = lf.astype(BF16)
    r1 = lf - hi.astype(F32)
    mid = r1.astype(BF16)
    lo = (r1 - mid.astype(F32)).astype(BF16)
    bcum = _dot(tril, hi) + _dot(tril, mid) + _dot(tril, lo)
    bcum_t = bcum.T
    g_t = g.T

    for h in range(M_HEADS):
        hs = slice(h * M_DH, (h + 1) * M_DH)
        b_col = bcum[:, M_HEADS + h:M_HEADS + h + 1]
        b_row = bcum_t[M_HEADS + h:M_HEADS + h + 1, :]
        li_col = g[:, h:h + 1]
        li_row = g_t[h:h + 1, :]
        m_prev = m_scr[h][:, 0:1]
        log_d = jnp.where(causal, b_col - b_row + li_row, NEG)
        log_inter = b_col + m_prev
        m_t = jnp.maximum(log_inter, jnp.max(log_d, axis=1, keepdims=True))
        d = jnp.exp(log_d - m_t)
        w_inter = jnp.exp(log_inter - m_t)
        qh = cq[:, hs]
        kh = cq[:, hw + h * M_DH:hw + (h + 1) * M_DH] * (M_DH ** -0.5)
        vh = v_ref[0][:, hs]
        qb = qh.astype(BF16)
        s = _dot_nt(qb, kh.astype(BF16)) * d
        ct_h = ct[h]
        num = w_inter * _dot(qb, ct_h.astype(BF16)) + _dot(s.astype(BF16), vh.astype(BF16))
        n_prev = n_scr[h:h + 1, :]
        den = w_inter * jnp.sum(qh * n_prev, axis=1, keepdims=True) + jnp.sum(s, axis=1, keepdims=True)
        hh = num / jnp.maximum(jnp.abs(den), jnp.exp(-m_t))
        hm_ref[0, :, hs] = jax.nn.sigmoid(mo_ref[0][:, hs]) * _rms_rows(hh, gmh_ref[...])
        m_new = m_t[L - 1:L, :]
        b_last = b_col[L - 1:L, :]
        w_c = jnp.exp(b_last + m_prev - m_new)
        w_s = jnp.exp(b_last - b_col + li_col - m_new)
        ct[h] = w_c * ct_h + _dot(kh.T.astype(BF16), (w_s * vh).astype(BF16))
        n_scr[h:h + 1, :] = w_c * n_prev + jnp.sum(w_s * kh, axis=0, keepdims=True)
        m_scr[h] = jnp.broadcast_to(m_new, (1, LANES))

    @pl.when(c_id == last)
    def _():
        for h in range(M_HEADS):
            c_ref[0, h] = ct[h].T
            m_ref[0, h:h + 1, :] = m_scr[h]
        n_ref[0] = n_scr[0:M_HEADS, :]


def mlstm_prefill(z, gif, w_conv, b_conv, bif_row, g_mh, chunk=256):
    b, t, _ = z.shape
    chunk = min(chunk, t)
    assert t % chunk == 0 and t >= CONV_W - 1
    nc = t // chunk
    hw = M_HEADS * M_DH

    def zspec(c):
        return pl.BlockSpec((1, chunk, SEC), lambda bi, ci: (bi, ci, c - C_MQ))

    def full(shape):
        return pl.BlockSpec(shape, lambda bi, ci: (0,) * len(shape))

    return pl.pallas_call(
        functools.partial(_mlstm_prefill_kernel, chunk=chunk),
        grid=(b, nc),
        in_specs=[zspec(C_MQ), zspec(C_MK), zspec(C_MV), zspec(C_MO),
                  pl.BlockSpec((1, chunk, LANES), lambda bi, ci: (bi, ci, 0)),
                  full((CONV_W, 2 * hw)), full((1, 2 * hw)), full((1, LANES)), full((1, M_DH))],
        out_specs=[pl.BlockSpec((1, chunk, hw), lambda bi, ci: (bi, ci, 0)),
                   pl.BlockSpec((1, M_HEADS, M_DH, M_DH), lambda bi, ci: (bi, 0, 0, 0)),
                   pl.BlockSpec((1, M_HEADS, M_DH), lambda bi, ci: (bi, 0, 0)),
                   pl.BlockSpec((1, M_HEADS, LANES), lambda bi, ci: (bi, 0, 0)),
                   pl.BlockSpec((1, CONV_W - 1, 2 * hw), lambda bi, ci: (bi, 0, 0))],
        out_shape=[jax.ShapeDtypeStruct((b, t, hw), F32),
                   jax.ShapeDtypeStruct((b, M_HEADS, M_DH, M_DH), F32),
                   jax.ShapeDtypeStruct((b, M_HEADS, M_DH), F32),
                   jax.ShapeDtypeStruct((b, M_HEADS, LANES), F32),
                   jax.ShapeDtypeStruct((b, CONV_W - 1, 2 * hw), F32)],
        scratch_shapes=[pltpu.VMEM((chunk + SUBLANES, 2 * hw), F32),
                        pltpu.VMEM((M_HEADS, M_DH, M_DH), F32),
                        pltpu.VMEM((SUBLANES, M_DH), F32),
                        pltpu.VMEM((M_HEADS, 1, LANES), F32)],
        compiler_params=_cparams("parallel", "arbitrary"), name="mlstm_prefill",
    )(z, z, z, z, gif, w_conv, b_conv, bif_row, g_mh)


def _mlstm_decode_kernel(zq_ref, zk_ref, v_ref, mo_ref, gif_ref, buf_ref, c0_ref, n0_ref, m0_ref,
                         wc_ref, bc_ref, bif_ref, gmh_ref, hm_ref, c_ref, n_ref, m_ref, conv_ref):
    hw = M_HEADS * M_DH
    u = jnp.concatenate([zq_ref[0], zk_ref[0]], axis=1)
    buf = buf_ref[0]
    y = bc_ref[...]
    for j in range(CONV_W - 1):
        y = y + buf[j:j + 1, :] * wc_ref[j:j + 1, :]
    y = y + u * wc_ref[CONV_W - 1:CONV_W, :]
    cq = y * jax.nn.sigmoid(y)
    conv_ref[0] = jnp.concatenate([buf[1:CONV_W - 1, :], u], axis=0)

    g = gif_ref[0] + bif_ref[...]
    lf_all = _log_sigmoid(g)
    eye = (lax.broadcasted_iota(jnp.int32, (M_DH, M_DH), 0)
           == lax.broadcasted_iota(jnp.int32, (M_DH, M_DH), 1))
    v_all = v_ref[0]
    mo = mo_ref[0]
    m0 = m0_ref[0]
    outs, m_out = [], []
    for h in range(M_HEADS):
        hs = slice(h * M_DH, (h + 1) * M_DH)
        li = g[:, h:h + 1]
        lf = lf_all[:, M_HEADS + h:M_HEADS + h + 1]
        m_prev = m0[:, h:h + 1]
        log_inter = lf + m_prev
        m_t = jnp.maximum(log_inter, li)
        d = jnp.exp(li - m_t)
        w_inter = jnp.exp(log_inter - m_t)
        qh = cq[:, hs]
        kh = cq[:, hw + h * M_DH:hw + (h + 1) * M_DH] * (M_DH ** -0.5)
        vh = v_all[:, hs]
        s = jnp.sum(qh * kh, axis=1, keepdims=True) * d
        c_prev = c0_ref[0, h]
        cq_col = jnp.sum(c_prev * qh, axis=1, keepdims=True)
        v_col = jnp.sum(jnp.where(eye, vh, 0.0), axis=1, keepdims=True)
        num_col = w_inter * cq_col + s * v_col
        n_prev = n0_ref[0, h:h + 1, :]
        den = w_inter * jnp.sum(n_prev * qh, axis=1, keepdims=True) + s
        h_col = num_col / jnp.maximum(jnp.abs(den), jnp.exp(-m_t))
        h_row = jnp.sum(jnp.where(eye, h_col, 0.0), axis=0, keepdims=True)
        outs.append(jax.nn.sigmoid(mo[:, hs]) * _rms_rows(h_row, gmh_ref[...]))
        c_ref[0, h] = w_inter * c_prev + d * (v_col * kh)
        n_ref[0, h:h + 1, :] = w_inter * n_prev + d * kh
        m_out.append(m_t)
    hm_ref[0] = jnp.concatenate(outs, axis=1)
    lane = lax.broadcasted_iota(jnp.int32, (1, LANES), 1)
    m_row = jnp.zeros((1, LANES), F32)
    for h in range(M_HEADS):
        m_row = jnp.where(lane == h, m_out[h], m_row)
    m_ref[0] = m_row


def mlstm_decode(z, gif, buf, c0, n0, m0, w_conv, b_conv, bif_row, g_mh):
    b = z.shape[0]
    hw = M_HEADS * M_DH

    def zspec(c):
        return pl.BlockSpec((1, 1, SEC), lambda bi: (bi, 0, c))

    def per_b(shape):
        return pl.BlockSpec((1,) + shape, lambda bi: (bi,) + (0,) * len(shape))

    def full(shape):
        return pl.BlockSpec(shape, lambda bi: (0,) * len(shape))

    return pl.pallas_call(
        _mlstm_decode_kernel, grid=(b,),
        in_specs=[zspec(C_MQ), zspec(C_MK), zspec(C_MV), zspec(C_MO), per_b((1, LANES)),
                  per_b((CONV_W - 1, 2 * hw)), per_b((M_HEADS, M_DH, M_DH)), per_b((M_HEADS, M_DH)),
                  per_b((1, LANES)),
                  full((CONV_W, 2 * hw)), full((1, 2 * hw)), full((1, LANES)), full((1, M_DH))],
        out_specs=[per_b((1, hw)), per_b((M_HEADS, M_DH, M_DH)), per_b((M_HEADS, M_DH)),
                   per_b((1, LANES)), per_b((CONV_W - 1, 2 * hw))],
        out_shape=[jax.ShapeDtypeStruct((b, 1, hw), F32),
                   jax.ShapeDtypeStruct((b, M_HEADS, M_DH, M_DH), F32),
                   jax.ShapeDtypeStruct((b, M_HEADS, M_DH), F32),
                   jax.ShapeDtypeStruct((b, 1, LANES), F32),
                   jax.ShapeDtypeStruct((b, CONV_W - 1, 2 * hw), F32)],
        compiler_params=_cparams("parallel"), name="mlstm_decode",
    )(z, z, z, z, gif, buf, c0, n0, m0, w_conv, b_conv, bif_row, g_mh)


def _cross_attn_kernel(q_ref, k_ref, v_ref, o_ref):
    q = q_ref[0]
    tq = q.shape[0]
    if tq < SUBLANES:
        q = jnp.broadcast_to(q[0:1, :], (SUBLANES, q.shape[1]))
    k = k_ref[0].astype(BF16)
    v = v_ref[0].astype(BF16)
    for h in range(X_HEADS):
        hs = slice(h * X_DH, (h + 1) * X_DH)
        s = _dot_nt(q[:, hs].astype(BF16), k[:, hs]) * (X_DH ** -0.5)
        p = jnp.exp(s - jnp.max(s, axis=-1, keepdims=True))
        p = p / jnp.sum(p, axis=-1, keepdims=True)
        o = _dot(p.astype(BF16), v[:, hs])
        o_ref[0, :, hs] = o[0:tq, :]


def cross_attn(z, mk, mv, tq=1024):
    b, t, zc = z.shape
    tq = min(tq, t)
    assert t % tq == 0
    n_mem = mk.shape[1]
    xq = zc // SEC - 1
    return pl.pallas_call(
        _cross_attn_kernel, grid=(b, t // tq),
        in_specs=[pl.BlockSpec((1, tq, SEC), lambda bi, qi: (bi, qi, xq)),
                  pl.BlockSpec((1, n_mem, SEC), lambda bi, qi: (bi, 0, 0)),
                  pl.BlockSpec((1, n_mem, SEC), lambda bi, qi: (bi, 0, 0))],
        out_specs=pl.BlockSpec((1, tq, SEC), lambda bi, qi: (bi, qi, 0)),
        out_shape=jax.ShapeDtypeStruct((b, t, SEC), F32),
        compiler_params=_cparams("parallel", "parallel"), name="cross_attn",
    )(z, mk, mv)


def _merge_kernel(x_ref, g_ref, oa_ref, hm_ref, ox_ref, wg_ref, wb_ref, wo_ref, o_ref):
    x = x_ref[...]
    d = x.shape[1]
    h = _rms_rows(x, g_ref[...]).astype(BF16)
    merged = None
    for i, br in enumerate((oa_ref, hm_ref, ox_ref)):
        gate = jax.nn.sigmoid(_dot(h, wg_ref[:, i * d:(i + 1) * d]))
        term = gate * _dot(br[...].astype(BF16), wb_ref[i])
        merged = term if merged is None else merged + term
    o_ref[...] = x + _dot(merged.astype(BF16), wo_ref[...])


def merge(x, g, oa, hm, ox, w_gate, w_branch, w_o, tm=512):
    t, d = x.shape
    tm = min(tm, t)
    assert t % tm == 0

    def rows(width):
        return pl.BlockSpec((tm, width), lambda i: (i, 0))

    def full(shape):
        return pl.BlockSpec(shape, lambda i: (0,) * len(shape))

    return pl.pallas_call(
        _merge_kernel, grid=(t // tm,),
        in_specs=[rows(d), full((1, d)), rows(SEC), rows(SEC), rows(SEC),
                  full(w_gate.shape), full(w_branch.shape), full(w_o.shape)],
        out_specs=rows(d), out_shape=jax.ShapeDtypeStruct((t, d), F32),
        compiler_params=_cparams("parallel"), name="merge",
    )(x, g, oa, hm, ox, w_gate, w_branch, w_o)


def _route_top2(h, wr_hi, wr_lo, b_router):
    h_hi = h.astype(BF16)
    h_lo = (h - h_hi.astype(F32)).astype(BF16)
    logits = _dot(h_hi, wr_hi) + (_dot(h_hi, wr_lo) + _dot(h_lo, wr_hi)) + b_router
    lane = lax.broadcasted_iota(jnp.int32, logits.shape, 1)
    logits = jnp.where(lane < N_EXPERTS, logits, -jnp.inf)
    v1 = jnp.max(logits, axis=-1, keepdims=True)
    i1 = jnp.min(jnp.where(logits == v1, lane, LANES), axis=-1, keepdims=True)
    rest = jnp.where(lane == i1, -jnp.inf, logits)
    v2 = jnp.max(rest, axis=-1, keepdims=True)
    i2 = jnp.min(jnp.where(rest == v2, lane, LANES), axis=-1, keepdims=True)
    e2 = jnp.exp(v2 - v1)
    den = 1.0 + e2
    gates = jnp.where(lane == i1, 1.0 / den, 0.0) + jnp.where(lane == i2, e2 / den, 0.0)
    return gates, jnp.logical_or(lane == i1, lane == i2)


def _swiglu(h, wg, wu, wd):
    gg = _dot(h, wg)
    uu = _dot(h, wu)
    return _dot(((gg * jax.nn.sigmoid(gg)) * uu).astype(BF16), wd)


def _ffn_kernel(*refs, routed, blk):
    if routed and blk:
        (x_ref, g_ref, wg_ref, wu_ref, wd_ref, wrh_ref, wrl_ref, br_ref, o_ref,
         h_scr, gate_scr, key_scr, keyt_scr, cnt_scr) = refs
    elif routed:
        x_ref, g_ref, wg_ref, wu_ref, wd_ref, wrh_ref, wrl_ref, br_ref, o_ref, h_scr, gate_scr = refs
    else:
        x_ref, g_ref, wg_ref, wu_ref, wd_ref, o_ref, h_scr = refs
    c = pl.program_id(1)
    tm = x_ref.shape[0]

    @pl.when(c == 0)
    def _():
        x = x_ref[...]
        h = _rms_rows(x, g_ref[...])
        h_scr[...] = h.astype(BF16)
        o_ref[...] = x
        if routed:
            gates, sel = _route_top2(h, wrh_ref[...], wrl_ref[...], br_ref[...])
            gate_scr[...] = gates
        if routed and blk:
            earlier = (lax.broadcasted_iota(jnp.int32, (tm, tm), 1)
                       < lax.broadcasted_iota(jnp.int32, (tm, tm), 0))
            sel01 = jnp.where(sel, 1.0, 0.0)
            rank = _dot(jnp.where(earlier, 1.0, 0.0).astype(BF16), sel01.astype(BF16))
            key = jnp.where(sel, rank, -1.0)
            key_scr[...] = key
            keyt_scr[...] = key.T
            cnt_scr[...] = jnp.sum(sel01, axis=0, keepdims=True)

    if not routed:
        o_ref[...] += _swiglu(h_scr[...], wg_ref[0], wu_ref[0], wd_ref[0])
        return
    lane = lax.broadcasted_iota(jnp.int32, gate_scr.shape, 1)
    gate_col = jnp.sum(jnp.where(lane == c, gate_scr[...], 0.0), axis=-1, keepdims=True)
    if not blk:
        o_ref[...] += gate_col * _swiglu(h_scr[...], wg_ref[0], wu_ref[0], wd_ref[0])
        return

    key_col = jnp.sum(jnp.where(lane == c, key_scr[...], 0.0), axis=-1, keepdims=True)
    key_row = keyt_scr[pl.ds(c, 1), :]
    lane1 = lax.broadcasted_iota(jnp.int32, cnt_scr.shape, 1)
    count = jnp.sum(jnp.where(lane1 == c, cnt_scr[...], 0.0)).astype(jnp.int32)
    half = tm // 2

    def run_block(first_rank, size):
        base = first_rank.astype(F32)
        slot_rows = lax.broadcasted_iota(jnp.int32, (size, tm), 0).astype(F32)
        slot_cols = lax.broadcasted_iota(jnp.int32, (tm, size), 1).astype(F32)
        pick = jnp.where(key_row - base == slot_rows, 1.0, 0.0).astype(BF16)
        h_sel = _dot(pick, h_scr[...]).astype(BF16)
        y = _swiglu(h_sel, wg_ref[0], wu_ref[0], wd_ref[0]).astype(BF16)
        place = jnp.where(key_col - base == slot_cols, 1.0, 0.0).astype(BF16)
        for r in range(2):
            rows = slice(r * half, (r + 1) * half)
            o_ref[rows, :] += gate_col[rows, :] * _dot(place[rows, :], y)

    small = blk // 2
    n_full = count // blk
    rem = count - n_full * blk

    def block(b, carry):
        run_block(b * blk, blk)
        return carry

    lax.fori_loop(0, n_full + (rem > small).astype(jnp.int32), block, 0)

    @pl.when(jnp.logical_and(rem > 0, rem <= small))
    def _():
        run_block(n_full * blk, small)


def ffn(x, g, w_gate, w_up, w_down, router=None, tm=512, blk=256):
    t, d = x.shape
    if router is not None:
        tm = 2 * tm
    tm = min(tm, t)
    assert t % tm == 0
    blk = blk if (router is not None and tm >= 2 * blk) else 0
    wg_arr, wg_map = w_gate
    wu_arr, wu_map = w_up
    n_c, f, _ = w_down.shape
    in_specs = [pl.BlockSpec((tm, d), lambda i, c: (i, 0)),
                pl.BlockSpec((1, d), lambda i, c: (0, 0)),
                pl.BlockSpec((1, d, f), lambda i, c: wg_map(c)),
                pl.BlockSpec((1, d, f), lambda i, c: wu_map(c)),
                pl.BlockSpec((1, f, d), lambda i, c: (c, 0, 0))]
    args = [x, g, wg_arr, wu_arr, w_down]
    scratch = [pltpu.VMEM((tm, d), BF16)]
    if router is not None:
        in_specs += [pl.BlockSpec((d, LANES), lambda i, c: (0, 0)),
                     pl.BlockSpec((d, LANES), lambda i, c: (0, 0)),
                     pl.BlockSpec((1, LANES), lambda i, c: (0, 0))]
        args += list(router)
        scratch.append(pltpu.VMEM((tm, LANES), F32))
        if blk:
            scratch += [pltpu.VMEM((tm, LANES), F32), pltpu.VMEM((LANES, tm), F32), pltpu.VMEM((1, LANES), F32)]
    return pl.pallas_call(
        functools.partial(_ffn_kernel, routed=router is not None, blk=blk), grid=(t // tm, n_c),
        in_specs=in_specs, out_specs=pl.BlockSpec((tm, d), lambda i, c: (i, 0)),
        out_shape=jax.ShapeDtypeStruct((t, d), F32), scratch_shapes=scratch,
        compiler_params=_cparams("parallel", "arbitrary"), name="ffn")(*args)


def _pad_lanes(a):
    return jnp.pad(a, ((0, 0),) * (a.ndim - 1) + ((0, LANES - a.shape[-1]),))


def kernel(x_prompt, x_sample, mem_prompt, cache_k, cache_v, page_table, cache_mem_k, cache_mem_v, state_C, state_n, state_m, state_conv, g_attn_norm, w_in, b_if, g_q, g_k, lam_q1, lam_k1, lam_q2, lam_k2, g_subln, w_conv, b_conv, g_mh, g_mem_norm, w_mem_kv, g_mq, g_mk, w_branch, w_o, g_ffn_norm, w_dense_gu, w_dense_down, w_router, b_router, w_moe_gu, w_moe_down):
    depth = w_in.shape[0]
    bp, tp, d = x_prompt.shape
    bs, ts, _ = x_sample.shape
    assert ts == 1, "the sample group decodes one token per sequence"
    n_mem = mem_prompt.shape[1]
    hw = M_HEADS * M_DH
    n_pool, page = cache_k.shape[1], cache_k.shape[2]
    cache_kt = jnp.transpose(cache_k, (0, 1, 3, 4, 5, 2)).reshape(depth, n_pool, SEC, page)
    cache_v2 = cache_v.reshape(depth, n_pool, page * A_HEADS, A_DV)

    o_aq, o_ak, o_av, o_mqk = 0, 512, 1024, 1536
    o_mv, o_mo, o_mif, o_xq, o_g = 2560, 3072, 3584, 3592, 4104

    yp = x_prompt.reshape(bp * tp, d)
    ys = x_sample.reshape(bs * ts, d)
    mem = mem_prompt.reshape(bp * n_mem, d)
    outs = {k: [] for k in ("mkp", "mvp", "cp", "np", "mp", "convp", "ks", "vs", "cs", "ns", "ms", "convs")}
    k_state = v_state = None
    ones = jnp.ones((SEC,), F32)
    for l in range(depth):
        lam_init = 0.8 - 0.6 * math.exp(-0.3 * l)
        lam = (jnp.exp(jnp.sum(lam_q1[l] * lam_k1[l])) - jnp.exp(jnp.sum(lam_q2[l] * lam_k2[l]))
               + lam_init).reshape(1).astype(F32)
        wl = w_in[l]
        w_main = jnp.concatenate([wl[:, o_aq:o_mif], wl[:, o_xq:o_g]], axis=1).astype(BF16)
        w_if = _pad_lanes(wl[:, o_mif:o_xq]).astype(BF16)
        w_gate = wl[:, o_g:].astype(BF16)
        q_scale = (A_DQK ** -0.5) * math.log2(math.e)
        gains = jnp.concatenate([jnp.tile(g_q[l], SEC // A_DQK) * q_scale, jnp.tile(g_k[l], SEC // A_DQK),
                                 ones, ones, ones, ones, ones, jnp.tile(g_mq[l], SEC // X_DH)])[None, :]
        modes = (A_DQK, A_DQK, 0, 0, 0, 0, 0, X_DH)
        g_attn = g_attn_norm[l][None, :]
        bif_row = _pad_lanes(b_if[l][None, :])
        wc, bc, gmh = w_conv[l], b_conv[l][None, :], g_mh[l][None, :]
        wb, wo = w_branch[l].astype(BF16), w_o[l].astype(BF16)
        out_scale = 1.0 - lam_init

        kv_gains = jnp.concatenate([jnp.tile(g_mk[l], SEC // X_DH), ones])[None, :]
        mkv, = norm_proj(mem, g_mem_norm[l][None, :], w_mem_kv[l].astype(BF16), kv_gains, (X_DH, 0))
        mk_p = mkv[:, :SEC].reshape(bp, n_mem, SEC)
        mv_p = mkv[:, SEC:].reshape(bp, n_mem, SEC)

        z, gif, qt, kb, vt, k_state, v_state = norm_proj(
            yp, g_attn, w_main, gains, modes, w_extra=w_if,
            prefill=(bp, tp, min(512, tp), l, depth, k_state, v_state))
        z3 = z.reshape(bp, tp, z.shape[1])
        oa = attn_prefill(qt, kb, vt, lam, g_subln[l], out_scale)
        hm, c_p, n_p, m_p, conv_p = mlstm_prefill(z3, gif.reshape(bp, tp, LANES), wc, bc, bif_row, gmh)
        ox = cross_attn(z3, mk_p, mv_p)
        yp = merge(yp, g_attn, oa.reshape(bp * tp, SEC), hm.reshape(bp * tp, hw), ox.reshape(bp * tp, SEC),
                   w_gate, wb, wo)

        zs, gifs = norm_proj(ys, g_attn, w_main, gains, modes, w_extra=w_if)
        zs3 = zs.reshape(bs, 1, Z_COLS)
        oa_s = attn_decode(zs3, cache_kt, cache_v2, page_table, l, lam, g_subln[l], out_scale)
        hm_s, c_s, n_s, m_s, conv_s = mlstm_decode(
            zs3, gifs.reshape(bs, 1, LANES), state_conv[l], state_C[l], state_n[l],
            _pad_lanes(state_m[l])[:, None, :], wc, bc, bif_row, gmh)
        ox_s = cross_attn(zs3, cache_mem_k[l].reshape(bs, n_mem, SEC), cache_mem_v[l].reshape(bs, n_mem, SEC))
        ys = merge(ys, g_attn, oa_s.reshape(bs, SEC), hm_s.reshape(bs, hw), ox_s.reshape(bs, SEC),
                   w_gate, wb, wo)
        ka_s = zs3[:, :, C_AK * SEC:(C_AK + 1) * SEC].reshape(bs, 1, A_HEADS, 2, A_DQK)
        va_s = zs3[:, :, C_AV * SEC:(C_AV + 1) * SEC].reshape(bs, 1, A_HEADS, A_DV)

        i = l // 2
        g_ffn = g_ffn_norm[l][None, :]
        if l % 2 == 0:
            wgu = w_dense_gu[i].astype(BF16)[None]
            d_ff = w_dense_down.shape[1]
            n_c = 2 if d_ff % (2 * LANES) == 0 else 1
            f = d_ff // n_c
            w_g = (wgu, lambda c: (0, 0, c))
            w_u = (wgu, lambda c, n_c=n_c: (0, 0, n_c + c))
            w_d = w_dense_down[i].astype(BF16).reshape(n_c, f, d)
            yp = ffn(yp, g_ffn, w_g, w_u, w_d)
            ys = ffn(ys, g_ffn, w_g, w_u, w_d)
        else:
            wgu = w_moe_gu[i].astype(BF16)
            w_g = (wgu, lambda c: (c, 0, 0))
            w_u = (wgu, lambda c: (c, 0, 1))
            w_d = w_moe_down[i].astype(BF16)
            wr = _pad_lanes(w_router[i])
            wr_hi = wr.astype(BF16)
            wr_lo = (wr - wr_hi.astype(F32)).astype(BF16)
            router = (wr_hi, wr_lo, _pad_lanes(b_router[i][None, :]))
            yp = ffn(yp, g_ffn, w_g, w_u, w_d, router)
            ys = ffn(ys, g_ffn, w_g, w_u, w_d, router)

        for name, val in zip(outs, (mk_p.reshape(bp, n_mem, X_HEADS, X_DH),
                                    mv_p.reshape(bp, n_mem, X_HEADS, X_DH), c_p, n_p, m_p[:, :, 0], conv_p,
                                    ka_s, va_s, c_s, n_s, m_s[:, 0, :M_HEADS], conv_s)):
            outs[name].append(val)

    st = {name: jnp.stack(vals, axis=0) for name, vals in outs.items()}
    st["kp"] = jnp.transpose(k_state.reshape(depth, bp, A_HEADS, 2, A_DQK, tp), (0, 1, 5, 2, 3, 4))
    st["vp"] = v_state.reshape(depth, bp, tp, A_HEADS, A_DV)
    return (yp.reshape(bp, tp, d), ys.reshape(bs, ts, d), st["kp"], st["vp"], st["mkp"], st["mvp"],
            st["cp"], st["np"], st["mp"], st["convp"], st["ks"], st["vs"], st["cs"], st["ns"], st["ms"],
            st["convs"])
```
